```python
import math
import jax
import jax.numpy as jnp
from jax import lax
import numpy as np

D_MODEL = 1024
BATCH = 16
SEQ = 256
DEPTH = 4
DEC_BATCH = 4
DEC_SEQ = 1024
PAST_LEN = 512

GRID_W = 64
HEAD_DIM = 64
NA_HEADS = 4
NA_WR = 8
NA_WC = 16
NA_QB = 16
NA_KSPAN = 32
ML_HEADS = 4
ML_DK = 64
ML_DV = 64
ML_CHUNK = 64
DN_HEADS = 4
DN_DK = 64
DN_DV = 64
DN_CONV = 5
DN_CHUNK = 64
SW_HEADS = 4
SW_KV_HEADS = 2
SW_GROUP = SW_HEADS // SW_KV_HEADS
SW_WINDOW = 128
SW_BLOCK = 128
Q_BLOCK = 128
ROPE_BASE = 10000.0
N_BRANCH = 4
FFN_DIM = 2816
N_EXPERTS = 8
TOP_K = 2
EXPERT_DIM = 2816
N_DENSE = (DEPTH + 1) // 2
N_MOE = DEPTH // 2
ALPHA = (2 * DEPTH) ** 0.25
BETA_INIT = (8 * DEPTH) ** -0.25
LN_EPS = 1e-5
NORM_EPS = 1e-6

NA_W = NA_HEADS * HEAD_DIM
ML_QK = ML_HEADS * ML_DK
ML_V = ML_HEADS * ML_DV
DN_QK = DN_HEADS * DN_DK
DN_V = DN_HEADS * DN_DV
SW_Q = SW_HEADS * HEAD_DIM
SW_KV = SW_KV_HEADS * HEAD_DIM
BRANCH_W = NA_W
IN_SIZES = (NA_W, NA_W, NA_W, ML_QK, ML_QK, ML_V, 4 * ML_HEADS, ML_V,
            2 * DN_QK + DN_V, 2 * DN_HEADS, 2 * DN_HEADS, DN_V,
            SW_Q, SW_KV, SW_KV, N_BRANCH * D_MODEL)
IN_SPLITS = tuple(int(s) for s in np.cumsum(IN_SIZES)[:-1])
W_IN = int(sum(IN_SIZES))
F32 = jnp.float32

kernel_name = 'hybrid_natten_mlstm_gdn_swa_moe_step'


def layer_norm(x, g, b):
    xf = x.astype(F32)
    mu = jnp.mean(xf, -1, keepdims=True)
    var = jnp.mean(jnp.square(xf - mu), -1, keepdims=True)
    return ((xf - mu) * lax.rsqrt(var + LN_EPS) * g.astype(F32) + b.astype(F32)).astype(x.dtype)


def modulate(x, shift, scale):
    return x * (1 + scale) + shift


def post_norm(x, gate, y, g, b):
    return layer_norm(ALPHA * x + gate * y, g, b)


def adaln(cvec, w, b):
    mod = jax.nn.silu(cvec) @ w + b
    return [m[:, None, :] for m in jnp.split(mod, 6, axis=-1)]


def rope_1d(x, pos):
    half = x.shape[-1] // 2
    freqs = ROPE_BASE ** (-jnp.arange(half, dtype=F32) / half)
    ang = pos.astype(F32)[:, None] * freqs[None, :]
    cos, sin = jnp.cos(ang)[:, None, :], jnp.sin(ang)[:, None, :]
    x1, x2 = x[..., :half].astype(F32), x[..., half:].astype(F32)
    return jnp.concatenate([x1 * cos - x2 * sin, x1 * sin + x2 * cos], -1).astype(x.dtype)


def axial_rope(x):
    t = jnp.arange(x.shape[1])
    half = x.shape[-1] // 2
    return jnp.concatenate([rope_1d(x[..., :half], t // GRID_W), rope_1d(x[..., half:], t % GRID_W)], -1)


def bhtd(x, n):
    b, t = x.shape[:2]
    return jnp.transpose(x.astype(F32).reshape(b, t, n, -1), (0, 2, 1, 3))


def rev(a):
    return jnp.flip(a, axis=2)


def to_chunks(a, size):
    b, h, t = a.shape[:3]
    return jnp.moveaxis(a.reshape((b, h, t // size, size) + a.shape[3:]), 2, 0)


def from_chunks(a):
    nc, b, h, size = a.shape[:4]
    return jnp.moveaxis(a, 0, 2).reshape((b, h, nc * size) + a.shape[4:])


def l2norm(x):
    return x * lax.rsqrt(jnp.sum(x * x, -1, keepdims=True) + NORM_EPS)


def short_conv(x, w):
    k = w.shape[0]
    return lax.conv_general_dilated(x, w[:, None, :], (1,), [(k // 2, k // 2)],
                                    dimension_numbers=('NWC', 'WIO', 'NWC'),
                                    feature_group_count=x.shape[-1])


def ctx_attention(q, k, v, sink):
    b, t, hkv, g, dh = q.shape
    scale = dh ** -0.5
    qb = jnp.moveaxis(q.reshape(b, t // Q_BLOCK, Q_BLOCK, hkv, g, dh), 1, 0)

    def one_block(qi):
        s = jnp.einsum('bqkgd,bmkd->bkgqm', qi, k).astype(F32) * scale
        if sink is not None:
            snk = jnp.broadcast_to(sink.astype(F32)[None, :, :, None, None], s.shape[:-1] + (1,))
            p = jax.nn.softmax(jnp.concatenate([s, snk], -1), -1)[..., :-1]
        else:
            p = jax.nn.softmax(s, -1)
        return jnp.einsum('bkgqm,bmkd->bqkgd', p.astype(v.dtype), v)

    out = lax.map(one_block, qb)
    return jnp.moveaxis(out, 0, 1).reshape(b, t, hkv * g * dh)


def na_latent(q, k, v, ck, cv, rpb):
    b, n, h, dh = q.shape
    rows = n // GRID_W
    wr = min(NA_WR, rows)
    nb = GRID_W // NA_QB
    r = np.arange(rows)
    rs = np.clip(r - wr // 2, 0, rows - wr)
    key_rows = rs[:, None] + np.arange(wr)[None, :]
    qcol = np.arange(nb)[:, None] * NA_QB + np.arange(NA_QB)[None, :]
    ks = np.clip(np.arange(nb) * NA_QB - NA_WC // 2, 0, GRID_W - NA_KSPAN)
    key_cols = ks[:, None] + np.arange(NA_KSPAN)[None, :]
    idx = (key_rows[:, None, :, None] * GRID_W + key_cols[None, :, None, :]).reshape(rows, nb, wr * NA_KSPAN)
    cs = np.clip(qcol - NA_WC // 2, 0, GRID_W - NA_WC)
    kc = key_cols[:, None, :]
    col_ok = (kc >= cs[..., None]) & (kc < cs[..., None] + NA_WC)
    dr = key_rows - r[:, None] + NA_WR - 1
    dc = np.clip(kc - qcol[..., None] + NA_WC - 1, 0, 2 * NA_WC - 2)
    bias = rpb.astype(F32)[:, dr[:, None, None, :, None], dc[None, :, :, None, :]]
    bias = jnp.where(col_ok[None, None, :, :, None, :], bias, -jnp.inf)
    bias = bias.reshape(h, rows, nb, NA_QB, wr * NA_KSPAN)
    qg = q.reshape(b, rows, nb, NA_QB, h, dh)
    kg = k[:, idx]
    vg = v[:, idx]
    scale = dh ** -0.5
    s_loc = jnp.einsum('brnqhd,brnlhd->bhrnql', qg, kg).astype(F32) * scale + bias[None]
    s_ctx = jnp.einsum('brnqhd,bmhd->bhrnqm', qg, ck).astype(F32) * scale
    n_loc = wr * NA_KSPAN
    p = jax.nn.softmax(jnp.concatenate([s_loc, s_ctx], -1), -1).astype(v.dtype)
    out = (jnp.einsum('bhrnql,brnlhd->brnqhd', p[..., :n_loc], vg)
           + jnp.einsum('bhrnqm,bmhd->brnqhd', p[..., n_loc:], cv))
    return out.reshape(b, n, h * dh)


def swa_latent(q, k, v, ck, cv, sink):
    b, n, hkv, g, dh = q.shape
    nb = n // SW_BLOCK
    qb = q.reshape(b, nb, SW_BLOCK, hkv, g, dh)
    pad = ((0, 0), (SW_BLOCK, SW_BLOCK), (0, 0), (0, 0))
    kp, vp = jnp.pad(k, pad), jnp.pad(v, pad)
    idx = np.arange(nb)[:, None] * SW_BLOCK + np.arange(3 * SW_BLOCK)[None, :]
    kb, vb = kp[:, idx], vp[:, idx]
    kpos = (idx - SW_BLOCK)[:, None, :]
    qpos = (np.arange(nb)[:, None] * SW_BLOCK + np.arange(SW_BLOCK)[None, :])[:, :, None]
    ok = (np.abs(qpos - kpos) <= SW_WINDOW) & (kpos >= 0) & (kpos < n)
    scale = dh ** -0.5
    s_loc = jnp.where(ok, jnp.einsum('bnqkgd,bnlkd->bkgnql', qb, kb).astype(F32) * scale, -jnp.inf)
    s_ctx = jnp.einsum('bnqkgd,bmkd->bkgnqm', qb, ck).astype(F32) * scale
    s_snk = jnp.broadcast_to(sink.astype(F32)[None, :, :, None, None, None], s_loc.shape[:-1] + (1,))
    p = jax.nn.softmax(jnp.concatenate([s_loc, s_ctx, s_snk], -1), -1)
    n_loc, m = 3 * SW_BLOCK, ck.shape[1]
    p_loc = p[..., :n_loc].astype(v.dtype)
    p_ctx = p[..., n_loc:n_loc + m].astype(v.dtype)
    out = (jnp.einsum('bkgnql,bnlkd->bnqkgd', p_loc, vb)
           + jnp.einsum('bkgnqm,bmkd->bnqkgd', p_ctx, cv))
    return out.reshape(b, n, hkv * g * dh)


def mlstm_chunked(q, k, v, li, lf, c0, n0, m0):
    size = ML_CHUNK
    xs = tuple(to_chunks(a, size) for a in (q, k, v, li, lf))
    tril = np.tril(np.ones((size, size), dtype=bool))

    def step(carry, inp):
        c, nrm, m = carry
        qc, kc, vc, ic, fc = inp
        bcum = jnp.cumsum(fc, axis=-1)
        d = jnp.where(tril, bcum[..., :, None] - bcum[..., None, :] + ic[..., None, :], -jnp.inf)
        inter = bcum + m[..., None]
        mt = jnp.maximum(inter, jnp.max(d, axis=-1))
        wd = jnp.exp(d - mt[..., None])
        wi = jnp.exp(inter - mt)
        s = jnp.einsum('bhtk,bhsk->bhts', qc, kc) * wd
        num = jnp.einsum('bhts,bhsv->bhtv', s, vc) + wi[..., None] * jnp.einsum('bhtk,bhkv->bhtv', qc, c)
        den = jnp.sum(s, -1) + wi * jnp.einsum('bhtk,bhk->bht', qc, nrm)
        hout = num / jnp.maximum(jnp.abs(den), jnp.exp(-mt))[..., None]
        b_last = bcum[..., -1]
        gw = b_last[..., None] - bcum + ic
        m_new = jnp.maximum(b_last + m, jnp.max(gw, -1))
        wg = jnp.exp(gw - m_new[..., None])
        wc = jnp.exp(b_last + m - m_new)
        c = wc[..., None, None] * c + jnp.einsum('bhs,bhsk,bhsv->bhkv', wg, kc, vc)
        nrm = wc[..., None] * nrm + jnp.einsum('bhs,bhsk->bhk', wg, kc)
        return (c, nrm, m_new), hout

    (c, nrm, m), hs = lax.scan(step, (c0, n0, m0), xs)
    return from_chunks(hs), c, nrm, m


def mlstm_branch(q, k, v, gates, o_pre, gate_b, norm_g, c0, n0, m0):
    b, t, _ = q.shape
    qh = bhtd(q, ML_HEADS)
    kh = bhtd(k, ML_HEADS) * ML_DK ** -0.5
    vh = bhtd(v, ML_HEADS)
    g = jnp.transpose(gates.astype(F32).reshape(b, t, 4, ML_HEADS) + gate_b.astype(F32), (0, 2, 3, 1))
    li = g[:, 0:2]
    lf = jax.nn.log_sigmoid(g[:, 2:4])
    c0, n0, m0 = c0.astype(F32), n0.astype(F32), m0.astype(F32)
    h_f, c_f, n_f, m_f = mlstm_chunked(qh, kh, vh, li[:, 0], lf[:, 0], c0[:, 0], n0[:, 0], m0[:, 0])
    h_b, c_b, n_b, m_b = mlstm_chunked(rev(qh), rev(kh), rev(vh), rev(li[:, 1]), rev(lf[:, 1]),
                                       c0[:, 1], n0[:, 1], m0[:, 1])
    hh = h_f + rev(h_b)
    mu = jnp.mean(hh, -1, keepdims=True)
    var = jnp.mean(jnp.square(hh - mu), -1, keepdims=True)
    hh = (hh - mu) * lax.rsqrt(var + LN_EPS)
    hh = jnp.transpose(hh, (0, 2, 1, 3)).reshape(b, t, ML_V) * norm_g.astype(F32)
    out = (hh * jax.nn.sigmoid(o_pre.astype(F32))).astype(q.dtype)
    return out, (jnp.stack([c_f, c_b], 1), jnp.stack([n_f, n_b], 1), jnp.stack([m_f, m_b], 1))


def gdn_chunked(q, k, v, g, beta, s0):
    size = DN_CHUNK
    q, k, v, g, beta = (to_chunks(a, size) for a in (q, k, v, g, beta))
    tril = np.tril(np.ones((size, size), dtype=bool))
    strict = np.tril(np.ones((size, size), dtype=bool), -1)
    gc = jnp.cumsum(g, axis=-1)
    decay = jnp.exp(jnp.where(tril, gc[..., :, None] - gc[..., None, :], -jnp.inf))
    kb = k * beta[..., None]
    a = jnp.where(strict, jnp.einsum('...id,...jd->...ij', kb, k) * decay, 0.0)
    rhs = jnp.concatenate([v * beta[..., None], kb * jnp.exp(gc)[..., None]], axis=-1)
    sol = lax.linalg.triangular_solve(a, rhs, left_side=True, lower=True, unit_diagonal=True)
    dv = v.shape[-1]
    u, w = sol[..., :dv], sol[..., dv:]
    qk = jnp.einsum('...id,...jd->...ij', q, k) * decay
    g_last = gc[..., -1]
    q_dec = q * jnp.exp(gc)[..., None]
    k_dec = k * jnp.exp(g_last[..., None] - gc)[..., None]

    def step(s, inp):
        u_c, w_c, qk_c, qd_c, kd_c, gl_c = inp
        v_new = u_c - jnp.einsum('bhlk,bhkv->bhlv', w_c, s)
        o = jnp.einsum('bhlk,bhkv->bhlv', qd_c, s) + jnp.einsum('bhij,bhjv->bhiv', qk_c, v_new)
        s = jnp.exp(gl_c)[..., None, None] * s + jnp.einsum('bhlk,bhlv->bhkv', kd_c, v_new)
        return s, o

    s, o = lax.scan(step, s0, (u, w, qk, q_dec, k_dec, g_last))
    return from_chunks(o), s


def gdn_branch(qkv, a_pre, b_pre, g_pre, a_log, dt_bias, conv_w, norm_g, s0):
    b, t, _ = qkv.shape
    x = jax.nn.silu(short_conv(qkv, conv_w).astype(F32))
    q, k, v = jnp.split(x, (DN_QK, 2 * DN_QK), axis=-1)
    q = l2norm(bhtd(q, DN_HEADS)) * DN_DK ** -0.5
    k = l2norm(bhtd(k, DN_HEADS))
    v = bhtd(v, DN_HEADS)
    a = jnp.transpose(a_pre.astype(F32).reshape(b, t, 2, DN_HEADS), (0, 2, 3, 1))
    bt = jnp.transpose(b_pre.astype(F32).reshape(b, t, 2, DN_HEADS), (0, 2, 3, 1))
    g = -jnp.exp(a_log.astype(F32))[None, :, :, None] * jax.nn.softplus(a + dt_bias.astype(F32)[None, :, :, None])
    beta = jax.nn.sigmoid(bt)
    s0 = s0.astype(F32)
    o_f, s_f = gdn_chunked(q, k, v, g[:, 0], beta[:, 0], s0[:, 0])
    o_b, s_b = gdn_chunked(rev(q), rev(k), rev(v), rev(g[:, 1]), rev(beta[:, 1]), s0[:, 1])
    o = o_f + rev(o_b)
    o = o * lax.rsqrt(jnp.mean(o * o, -1, keepdims=True) + NORM_EPS) * norm_g.astype(F32)
    o = jnp.transpose(o, (0, 2, 1, 3)) * jax.nn.silu(g_pre.astype(F32).reshape(b, t, DN_HEADS, DN_DV))
    return o.reshape(b, t, DN_V).astype(qkv.dtype), jnp.stack([s_f, s_b], 1)


def merge_branches(outs, gate_pre, w_branch, w_out):
    b, t = gate_pre.shape[:2]
    o = jnp.stack(outs, axis=2)
    y = jnp.einsum('btnw,nwd->btnd', o, w_branch)
    gates = jax.nn.sigmoid(gate_pre.astype(F32)).reshape(b, t, N_BRANCH, D_MODEL).astype(y.dtype)
    return jnp.sum(gates * y, axis=2) @ w_out


def context_mixer(h, w_in, ml_gate_b, ml_norm_g, dn_conv_w, dn_a_log, dn_dt_bias, dn_norm_g,
                  sw_sink, w_branch, w_out):
    b, t, _ = h.shape
    (na_q, na_k, na_v, ml_q, ml_k, ml_v, ml_g, ml_o, dn_qkv, dn_a, dn_b, dn_g,
     sw_q, sw_k, sw_v, mg) = jnp.split(h @ w_in, IN_SPLITS, axis=-1)
    nk = na_k.reshape(b, t, NA_HEADS, HEAD_DIM)
    nv = na_v.reshape(b, t, NA_HEADS, HEAD_DIM)
    o_na = ctx_attention(na_q.reshape(b, t, NA_HEADS, 1, HEAD_DIM), nk, nv, None)
    zc = jnp.zeros((b, 2, ML_HEADS, ML_DK, ML_DV), F32)
    zn = jnp.zeros((b, 2, ML_HEADS, ML_DK), F32)
    zm = jnp.zeros((b, 2, ML_HEADS), F32)
    o_ml, (mc, mn, mm) = mlstm_branch(ml_q, ml_k, ml_v, ml_g, ml_o, ml_gate_b, ml_norm_g, zc, zn, zm)
    zs = jnp.zeros((b, 2, DN_HEADS, DN_DK, DN_DV), F32)
    o_dn, ds = gdn_branch(dn_qkv, dn_a, dn_b, dn_g, dn_a_log, dn_dt_bias, dn_conv_w, dn_norm_g, zs)
    sk = sw_k.reshape(b, t, SW_KV_HEADS, HEAD_DIM)
    sv = sw_v.reshape(b, t, SW_KV_HEADS, HEAD_DIM)
    o_sw = ctx_attention(sw_q.reshape(b, t, SW_KV_HEADS, SW_GROUP, HEAD_DIM), sk, sv,
                         sw_sink.reshape(SW_KV_HEADS, SW_GROUP))
    out = merge_branches((o_na, o_ml, o_dn, o_sw), mg, w_branch, w_out)
    return out, (nk, nv, sk, sv, mc, mn, mm, ds)


def latent_mixer(h, na_k_c, na_v_c, sw_k_c, sw_v_c, ml_c, ml_n, ml_m, dn_s, w_in, na_rpb, ml_gate_b,
                 ml_norm_g, dn_conv_w, dn_a_log, dn_dt_bias, dn_norm_g, sw_sink, w_branch, w_out):
    b, t, _ = h.shape
    (na_q, na_k, na_v, ml_q, ml_k, ml_v, ml_g, ml_o, dn_qkv, dn_a, dn_b, dn_g,
     sw_q, sw_k, sw_v, mg) = jnp.split(h @ w_in, IN_SPLITS, axis=-1)
    o_na = na_latent(na_q.reshape(b, t, NA_HEADS, HEAD_DIM), na_k.reshape(b, t, NA_HEADS, HEAD_DIM),
                     na_v.reshape(b, t, NA_HEADS, HEAD_DIM), na_k_c, na_v_c, na_rpb)
    o_ml, _ = mlstm_branch(ml_q, ml_k, ml_v, ml_g, ml_o, ml_gate_b, ml_norm_g, ml_c, ml_n, ml_m)
    o_dn, _ = gdn_branch(dn_qkv, dn_a, dn_b, dn_g, dn_a_log, dn_dt_bias, dn_conv_w, dn_norm_g, dn_s)
    sq = axial_rope(sw_q.reshape(b, t, SW_HEADS, HEAD_DIM)).reshape(b, t, SW_KV_HEADS, SW_GROUP, HEAD_DIM)
    sk = axial_rope(sw_k.reshape(b, t, SW_KV_HEADS, HEAD_DIM))
    sv = sw_v.reshape(b, t, SW_KV_HEADS, HEAD_DIM)
    o_sw = swa_latent(sq, sk, sv, sw_k_c, sw_v_c, sw_sink.reshape(SW_KV_HEADS, SW_GROUP))
    return merge_branches((o_na, o_ml, o_dn, o_sw), mg, w_branch, w_out)


def swiglu(h, w_in, w_out):
    gate, up = jnp.split(h @ w_in, 2, axis=-1)
    return (jax.nn.silu(gate) * up) @ w_out


def moe_ffn(h, w_router, b_router, w_in, w_out):
    logits = (h @ w_router).astype(F32) + b_router.astype(F32)
    top_v, top_i = lax.top_k(logits, TOP_K)
    probs = jax.nn.softmax(top_v, axis=-1)
    gates = jnp.sum(jax.nn.one_hot(top_i, N_EXPERTS, dtype=F32) * probs[..., None], axis=-2).astype(h.dtype)
    out = jnp.zeros(h.shape[:-1] + (w_out.shape[-1],), h.dtype)
    for e in range(N_EXPERTS):
        out = out + gates[..., e:e + 1] * swiglu(h, w_in[e], w_out[e])
    return out


def channel_mixer(l, h, ffn_w_in, ffn_w_out, moe_router, moe_router_b, moe_w_in, moe_w_out):
    j = l // 2
    if l % 2 == 0:
        return swiglu(h, ffn_w_in[j], ffn_w_out[j])
    return moe_ffn(h, moe_router[j], moe_router_b[j], moe_w_in[j], moe_w_out[j])


def setup_inputs(seed: int = 0) -> dict:
    key = jax.random.key(seed)
    ks = iter(jax.random.split(key, 48))
    D = D_MODEL

    def nrm(shape, s):
        return jax.random.normal(next(ks), shape, F32) * s

    x_prompt = nrm((BATCH, SEQ, D), 1.0)
    x_sample = nrm((DEC_BATCH, DEC_SEQ, D), 1.0)
    c = nrm((DEC_BATCH, D), 1.0)
    cache_na_k = nrm((DEC_BATCH, DEPTH, PAST_LEN, NA_HEADS, HEAD_DIM), 1.0)
    cache_na_v = nrm((DEC_BATCH, DEPTH, PAST_LEN, NA_HEADS, HEAD_DIM), 1.0)
    cache_sw_k = nrm((DEC_BATCH, DEPTH, PAST_LEN, SW_KV_HEADS, HEAD_DIM), 1.0)
    cache_sw_v = nrm((DEC_BATCH, DEPTH, PAST_LEN, SW_KV_HEADS, HEAD_DIM), 1.0)
    state_ml_c = nrm((DEC_BATCH, DEPTH, 2, ML_HEADS, ML_DK, ML_DV), 0.5)
    state_ml_n = nrm((DEC_BATCH, DEPTH, 2, ML_HEADS, ML_DK), 0.5)
    state_ml_m = nrm((DEC_BATCH, DEPTH, 2, ML_HEADS), 1.0)
    state_dn_s = nrm((DEC_BATCH, DEPTH, 2, DN_HEADS, DN_DK, DN_DV), 0.3)
    c_ctx = nrm((D,), 1.0)
    ada_w = nrm((DEPTH, D, 6 * D), D ** -0.5)
    ada_b = nrm((DEPTH, 6 * D), 0.02)
    w_in = nrm((DEPTH, D, W_IN), D ** -0.5)
    na_rpb = nrm((DEPTH, NA_HEADS, 2 * NA_WR - 1, 2 * NA_WC - 1), 0.1)
    ml_gate_b = jnp.concatenate(
        [nrm((DEPTH, 2, ML_HEADS), 0.1),
         3.0 + 3.0 * jax.random.uniform(next(ks), (DEPTH, 2, ML_HEADS), F32)], axis=1)
    ml_norm_g = 1.0 + nrm((DEPTH, ML_V), 0.02)
    dn_conv_w = nrm((DEPTH, DN_CONV, 2 * DN_QK + DN_V), DN_CONV ** -0.5)
    dn_a_log = jnp.log(jax.random.uniform(next(ks), (DEPTH, 2, DN_HEADS), F32, 1.0, 16.0))
    dt = jnp.exp(jax.random.uniform(next(ks), (DEPTH, 2, DN_HEADS), F32, math.log(1e-3), math.log(1e-1)))
    dn_dt_bias = dt + jnp.log(-jnp.expm1(-dt))
    dn_norm_g = 1.0 + nrm((DEPTH, DN_DV), 0.02)
    sw_sink = nrm((DEPTH, SW_HEADS), 0.5)
    w_branch = nrm((DEPTH, N_BRANCH, BRANCH_W, D), BRANCH_W ** -0.5)
    w_out = nrm((DEPTH, D, D), D ** -0.5 * BETA_INIT)
    ln_g = 1.0 + nrm((DEPTH, 2, D), 0.02)
    ln_b = nrm((DEPTH, 2, D), 0.02)
    ffn_w_in = nrm((N_DENSE, D, 2 * FFN_DIM), D ** -0.5)
    ffn_w_out = nrm((N_DENSE, FFN_DIM, D), FFN_DIM ** -0.5 * BETA_INIT)
    moe_router = nrm((N_MOE, D, N_EXPERTS), D ** -0.5)
    moe_router_b = nrm((N_MOE, N_EXPERTS), 0.01)
    moe_w_in = nrm((N_MOE, N_EXPERTS, D, 2 * EXPERT_DIM), D ** -0.5)
    moe_w_out = nrm((N_MOE, N_EXPERTS, EXPERT_DIM, D), EXPERT_DIM ** -0.5 * BETA_INIT)
    return {'x_prompt': x_prompt, 'x_sample': x_sample, 'c': c,
            'cache_na_k': cache_na_k, 'cache_na_v': cache_na_v,
            'cache_sw_k': cache_sw_k, 'cache_sw_v': cache_sw_v,
            'state_ml_c': state_ml_c, 'state_ml_n': state_ml_n, 'state_ml_m': state_ml_m,
            'state_dn_s': state_dn_s, 'c_ctx': c_ctx, 'ada_w': ada_w, 'ada_b': ada_b,
            'w_in': w_in, 'na_rpb': na_rpb, 'ml_gate_b': ml_gate_b, 'ml_norm_g': ml_norm_g,
            'dn_conv_w': dn_conv_w, 'dn_a_log': dn_a_log, 'dn_dt_bias': dn_dt_bias,
            'dn_norm_g': dn_norm_g, 'sw_sink': sw_sink, 'w_branch': w_branch, 'w_out': w_out,
            'ln_g': ln_g, 'ln_b': ln_b, 'ffn_w_in': ffn_w_in, 'ffn_w_out': ffn_w_out,
            'moe_router': moe_router, 'moe_router_b': moe_router_b,
            'moe_w_in': moe_w_in, 'moe_w_out': moe_w_out}


def reference(x_prompt, x_sample, c, cache_na_k, cache_na_v, cache_sw_k, cache_sw_v,
              state_ml_c, state_ml_n, state_ml_m, state_dn_s, c_ctx, ada_w, ada_b, w_in, na_rpb,
              ml_gate_b, ml_norm_g, dn_conv_w, dn_a_log, dn_dt_bias, dn_norm_g, sw_sink,
              w_branch, w_out, ln_g, ln_b, ffn_w_in, ffn_w_out, moe_router, moe_router_b,
              moe_w_in, moe_w_out):
    xp, xs = x_prompt, x_sample
    st_lists = [[] for _ in range(8)]
    for l in range(DEPTH):
        sh1, sc1, g1, sh2, sc2, g2 = adaln(c_ctx[None, :], ada_w[l], ada_b[l])
        mix, st = context_mixer(modulate(xp, sh1, sc1), w_in[l], ml_gate_b[l], ml_norm_g[l],
                                dn_conv_w[l], dn_a_log[l], dn_dt_bias[l], dn_norm_g[l], sw_sink[l],
                                w_branch[l], w_out[l])
        xp = post_norm(xp, g1, mix, ln_g[l, 0], ln_b[l, 0])
        ff = channel_mixer(l, modulate(xp, sh2, sc2), ffn_w_in, ffn_w_out, moe_router, moe_router_b,
                           moe_w_in, moe_w_out)
        xp = post_norm(xp, g2, ff, ln_g[l, 1], ln_b[l, 1])
        for lst, a in zip(st_lists, st):
            lst.append(a)
        sh1, sc1, g1, sh2, sc2, g2 = adaln(c, ada_w[l], ada_b[l])
        mix = latent_mixer(modulate(xs, sh1, sc1), cache_na_k[:, l], cache_na_v[:, l],
                           cache_sw_k[:, l], cache_sw_v[:, l], state_ml_c[:, l], state_ml_n[:, l],
                           state_ml_m[:, l], state_dn_s[:, l], w_in[l], na_rpb[l], ml_gate_b[l],
                           ml_norm_g[l], dn_conv_w[l], dn_a_log[l], dn_dt_bias[l], dn_norm_g[l],
                           sw_sink[l], w_branch[l], w_out[l])
        xs = post_norm(xs, g1, mix, ln_g[l, 0], ln_b[l, 0])
        ff = channel_mixer(l, modulate(xs, sh2, sc2), ffn_w_in, ffn_w_out, moe_router, moe_router_b,
                           moe_w_in, moe_w_out)
        xs = post_norm(xs, g2, ff, ln_g[l, 1], ln_b[l, 1])
    (new_na_k, new_na_v, new_sw_k, new_sw_v,
     new_ml_c, new_ml_n, new_ml_m, new_dn_s) = [jnp.stack(s, axis=1) for s in st_lists]
    return (xp, xs, new_na_k, new_na_v, new_sw_k, new_sw_v, new_ml_c, new_ml_n, new_ml_m, new_dn_s)
```

```python
import functools
import math

import numpy as np
import jax
import jax.numpy as jnp
from jax import lax
from jax.experimental import pallas as pl
from jax.experimental.pallas import tpu as pltpu

F32 = jnp.float32
BF16 = jnp.bfloat16
HIGHEST = lax.Precision.HIGHEST

D_MODEL = 1024
BATCH = 16
SEQ = 256
DEPTH = 4
DEC_BATCH = 4
DEC_SEQ = 1024
PAST_LEN = 512
GRID_W = 64
HEAD_DIM = 64
NA_HEADS = 4
NA_WR = 8
NA_WC = 16
ML_HEADS = 4
DN_HEADS = 4
DN_CONV = 5
SW_HEADS = 4
SW_KV_HEADS = 2
SW_WINDOW = 128
SW_BLOCK = 128
ROPE_BASE = 10000.0
N_BRANCH = 4
FFN_DIM = 2816
N_EXPERTS = 8
ALPHA = (2 * DEPTH) ** 0.25
LN_EPS = 1e-5
NORM_EPS = 1e-6

N_CTX = BATCH * SEQ
N_LAT = DEC_BATCH * DEC_SEQ
N_TOK = N_CTX + N_LAT
BRANCH_W = NA_HEADS * HEAD_DIM

P_NA = 0
P_ML = 768
P_DN = 1536
P_MLO = 2304
P_DNG = 2560
P_SWQ = 2816
P_SWK = 3072
P_SWV = 3200
P_GATE = 3328
P_W = 3584

SEQ_BLK = 256
V7X_VMEM_LIMIT = 56 * 1024 * 1024


def _cparams(sem, vmem=V7X_VMEM_LIMIT):
    return pltpu.CompilerParams(dimension_semantics=sem, vmem_limit_bytes=vmem)


def _dot(a, b):
    return jnp.dot(a.astype(BF16), b.astype(BF16), preferred_element_type=F32)


def _dot_nt(a, b):
    return lax.dot_general(a.astype(BF16), b.astype(BF16), (((1,), (1,)), ((), ())),
                           preferred_element_type=F32)


def _dot_tn(a, b):
    return lax.dot_general(a.astype(BF16), b.astype(BF16), (((0,), (0,)), ((), ())),
                           preferred_element_type=F32)


def _dot_f32(a, b):
    return jnp.dot(a, b, precision=HIGHEST, preferred_element_type=F32)


def _sigmoid(x):
    return 1.0 / (1.0 + jnp.exp(-x))


def _softplus(x):
    return jnp.maximum(x, 0.0) + jnp.log(1.0 + jnp.exp(-jnp.abs(x)))


def _log_sigmoid(x):
    return -_softplus(-x)


def _mod_row(tok_start):
    return jnp.where(tok_start < N_CTX, 0, 1 + (tok_start - N_CTX) // DEC_SEQ)


def _layer_norm(z, g, b):
    mu = jnp.mean(z, axis=-1, keepdims=True)
    zc = z - mu
    var = jnp.mean(zc * zc, axis=-1, keepdims=True)
    return zc * lax.rsqrt(var + LN_EPS) * g + b


def _ada_kernel(c_ref, w_ref, b_ref, o_ref):
    c = c_ref[...]
    s = c * _sigmoid(c)
    o_ref[...] = _dot(s, w_ref[...]) + b_ref[...]


def _ada(cvec, ada_w, ada_b):
    tn = 1536
    n = 6 * D_MODEL
    return pl.pallas_call(
        _ada_kernel,
        out_shape=jax.ShapeDtypeStruct((DEPTH, 8, n), F32),
        grid=(DEPTH, n // tn),
        in_specs=[
            pl.BlockSpec((8, D_MODEL), lambda l, j: (0, 0)),
            pl.BlockSpec((None, D_MODEL, tn), lambda l, j: (l, 0, j)),
            pl.BlockSpec((None, 1, tn), lambda l, j: (l, 0, j)),
        ],
        out_specs=pl.BlockSpec((None, 8, tn), lambda l, j: (l, 0, j)),
        compiler_params=_cparams(("parallel", "parallel")),
        name="ada",
    )(cvec, ada_w, ada_b.reshape(DEPTH, 1, n))


def _mod_spec(chunk, tm):
    return pl.BlockSpec((None, None, 1, D_MODEL), lambda i, *_: (_mod_row(i * tm), chunk, 0, 0))


IN_TM = 512
IN_NCH = 4


def _inproj_kernel(x_ref, sh_ref, sc_ref, w_ref, wt_ref, p_ref, gt_ref):
    xm = (x_ref[...] * (1.0 + sc_ref[...]) + sh_ref[...]).astype(BF16)
    cw = P_W // IN_NCH
    for c in range(IN_NCH):
        p_ref[:, c * cw:(c + 1) * cw] = jnp.dot(xm, w_ref[:, c * cw:(c + 1) * cw],
                                                preferred_element_type=F32)
    gt = lax.dot_general(wt_ref[...], xm, (((1,), (1,)), ((), ())), preferred_element_type=F32)
    gt_ref[0] = gt[0:16]
    gt_ref[1] = gt[16:32]


def _inproj(x, mod_l, w1, wt):
    tm = IN_TM
    return pl.pallas_call(
        _inproj_kernel,
        out_shape=(jax.ShapeDtypeStruct((N_TOK, P_W), F32),
                   jax.ShapeDtypeStruct((2, 16, N_TOK), F32)),
        grid=(N_TOK // tm,),
        in_specs=[
            pl.BlockSpec((tm, D_MODEL), lambda i: (i, 0)),
            _mod_spec(0, tm),
            _mod_spec(1, tm),
            pl.BlockSpec((D_MODEL, P_W), lambda i: (0, 0)),
            pl.BlockSpec((32, D_MODEL), lambda i: (0, 0)),
        ],
        out_specs=(pl.BlockSpec((tm, P_W), lambda i: (i, 0)),
                   pl.BlockSpec((2, 16, tm), lambda i: (0, 0, i))),
        compiler_params=_cparams(("parallel",)),
        name="inproj",
    )(x, mod_l, mod_l, w1, wt)


def _ctx_attn_kernel(sink_ref, na_ref, q_ref, k_ref, v_ref, ona_ref, osw_ref):
    scale = HEAD_DIM ** -0.5
    for h in range(NA_HEADS):
        q = na_ref[:, h * 64:(h + 1) * 64]
        k = na_ref[:, 256 + h * 64:256 + (h + 1) * 64]
        v = na_ref[:, 512 + h * 64:512 + (h + 1) * 64]
        s = _dot_nt(q, k) * scale
        m = jnp.max(s, axis=1, keepdims=True)
        p = jnp.exp(s - m)
        l = jnp.sum(p, axis=1, keepdims=True)
        ona_ref[:, h * 64:(h + 1) * 64] = _dot(p, v) / l
    for h in range(SW_HEADS):
        kh = h // (SW_HEADS // SW_KV_HEADS)
        q = q_ref[:, h * 64:(h + 1) * 64]
        k = k_ref[:, kh * 64:(kh + 1) * 64]
        v = v_ref[:, kh * 64:(kh + 1) * 64]
        snk = sink_ref[h]
        s = _dot_nt(q, k) * scale
        m = jnp.maximum(jnp.max(s, axis=1, keepdims=True), snk)
        p = jnp.exp(s - m)
        l = jnp.sum(p, axis=1, keepdims=True) + jnp.exp(snk - m)
        osw_ref[:, h * 64:(h + 1) * 64] = _dot(p, v) / l


def _ctx_attn(p, sink):
    t = SEQ
    return pl.pallas_call(
        _ctx_attn_kernel,
        out_shape=(jax.ShapeDtypeStruct((N_CTX, BRANCH_W), F32),
                   jax.ShapeDtypeStruct((N_CTX, BRANCH_W), F32)),
        grid=(BATCH,),
        in_specs=[
            pl.BlockSpec(memory_space=pltpu.SMEM),
            pl.BlockSpec((t, 768), lambda b: (b, P_NA // 768)),
            pl.BlockSpec((t, 256), lambda b: (b, P_SWQ // 256)),
            pl.BlockSpec((t, 128), lambda b: (b, P_SWK // 128)),
            pl.BlockSpec((t, 128), lambda b: (b, P_SWV // 128)),
        ],
        out_specs=(pl.BlockSpec((t, BRANCH_W), lambda b: (b, 0)),
                   pl.BlockSpec((t, BRANCH_W), lambda b: (b, 0))),
        compiler_params=_cparams(("parallel",)),
        name="ctx_attn",
    )(sink, p, p, p, p)


NA_ROWS = DEC_SEQ // GRID_W
NA_KEYS = NA_WR * GRID_W


def _lat_na_kernel(q_ref, k_ref, v_ref, ck_ref, cv_ref, bias_ref, o_ref):
    scale = HEAD_DIM ** -0.5
    r = pl.program_id(1)
    rs = jnp.clip(r - NA_WR // 2, 0, NA_ROWS - NA_WR)
    start = pl.multiple_of(rs * GRID_W, GRID_W)
    kw = k_ref[pl.ds(start, NA_KEYS), :]
    vw = v_ref[pl.ds(start, NA_KEYS), :]
    for h in range(NA_HEADS):
        sl = slice(h * 64, (h + 1) * 64)
        q = q_ref[:, sl]
        s_loc = _dot_nt(q, kw[:, sl]) * scale + bias_ref[h]
        s_ctx = _dot_nt(q, ck_ref[:, sl]) * scale
        m = jnp.maximum(jnp.max(s_loc, axis=1, keepdims=True), jnp.max(s_ctx, axis=1, keepdims=True))
        p_loc = jnp.exp(s_loc - m)
        p_ctx = jnp.exp(s_ctx - m)
        l = jnp.sum(p_loc, axis=1, keepdims=True) + jnp.sum(p_ctx, axis=1, keepdims=True)
        o_ref[:, sl] = (_dot(p_loc, vw[:, sl]) + _dot(p_ctx, cv_ref[:, sl])) / l


def _lat_na(p, ck, cv, bias):
    qb0 = N_CTX // GRID_W
    kb0 = N_CTX // DEC_SEQ
    return pl.pallas_call(
        _lat_na_kernel,
        out_shape=jax.ShapeDtypeStruct((N_LAT, BRANCH_W), F32),
        grid=(DEC_BATCH, NA_ROWS),
        in_specs=[
            pl.BlockSpec((GRID_W, 256), lambda b, r: (qb0 + b * NA_ROWS + r, 0)),
            pl.BlockSpec((DEC_SEQ, 256), lambda b, r: (kb0 + b, 1)),
            pl.BlockSpec((DEC_SEQ, 256), lambda b, r: (kb0 + b, 2)),
            pl.BlockSpec((None, PAST_LEN, 256), lambda b, r: (b, 0, 0)),
            pl.BlockSpec((None, PAST_LEN, 256), lambda b, r: (b, 0, 0)),
            pl.BlockSpec((NA_HEADS, None, GRID_W, NA_KEYS), lambda b, r: (0, r, 0, 0)),
        ],
        out_specs=pl.BlockSpec((GRID_W, BRANCH_W), lambda b, r: (b * NA_ROWS + r, 0)),
        compiler_params=_cparams(("parallel", "arbitrary")),
        name="lat_na",
    )(p, p, p, ck, cv, bias)


def _na_bias_table(rpb):
    r = np.arange(NA_ROWS)
    rs = np.clip(r - NA_WR // 2, 0, NA_ROWS - NA_WR)
    dr = rs[:, None] + np.arange(NA_WR)[None, :] - r[:, None] + NA_WR - 1
    qc = np.arange(GRID_W)[:, None]
    kc = np.arange(GRID_W)[None, :]
    cs = np.clip(qc - NA_WC // 2, 0, GRID_W - NA_WC)
    ok = (kc >= cs) & (kc < cs + NA_WC)
    dc = np.clip(kc - qc + NA_WC - 1, 0, 2 * NA_WC - 2)
    b = rpb[:, dr[:, None, :, None], dc[None, :, None, :]]
    b = jnp.where(ok[None, None, :, None, :], b, -jnp.inf)
    return b.reshape(NA_HEADS, NA_ROWS, GRID_W, NA_KEYS)


SW_NB = DEC_SEQ // SW_BLOCK
SW_KEYS = 3 * SW_BLOCK


def _rope(x, cos, sin):
    w = x.shape[1]
    lane = lax.broadcasted_iota(jnp.int32, x.shape, 1)
    first = (lane % 32) < 16
    swapped = jnp.where(first, pltpu.roll(x, w - 16, 1), pltpu.roll(x, 16, 1))
    return x * cos + swapped * sin


def _lat_swa_kernel(sink_ref, q_ref, k_ref, v_ref, ck_ref, cv_ref, cos_ref, sin_ref, o_ref):
    scale = HEAD_DIM ** -0.5
    n = pl.program_id(1)
    q0 = pl.multiple_of(n * SW_BLOCK, SW_BLOCK)
    start = pl.multiple_of(jnp.clip((n - 1) * SW_BLOCK, 0, DEC_SEQ - SW_KEYS), SW_BLOCK)
    q = _rope(q_ref[...], cos_ref[pl.ds(q0, SW_BLOCK), :], sin_ref[pl.ds(q0, SW_BLOCK), :])
    kw = _rope(k_ref[pl.ds(start, SW_KEYS), :], cos_ref[pl.ds(start, SW_KEYS), 0:128],
               sin_ref[pl.ds(start, SW_KEYS), 0:128])
    vw = v_ref[pl.ds(start, SW_KEYS), :]
    qpos = q0 + lax.broadcasted_iota(jnp.int32, (SW_BLOCK, SW_KEYS), 0)
    kpos = start + lax.broadcasted_iota(jnp.int32, (SW_BLOCK, SW_KEYS), 1)
    ok = jnp.abs(qpos - kpos) <= SW_WINDOW
    for h in range(SW_HEADS):
        kh = h // (SW_HEADS // SW_KV_HEADS)
        ksl = slice(kh * 64, (kh + 1) * 64)
        qh = q[:, h * 64:(h + 1) * 64]
        snk = sink_ref[h]
        s_loc = jnp.where(ok, _dot_nt(qh, kw[:, ksl]) * scale, -jnp.inf)
        s_ctx = _dot_nt(qh, ck_ref[:, ksl]) * scale
        m = jnp.maximum(jnp.maximum(jnp.max(s_loc, axis=1, keepdims=True),
                                    jnp.max(s_ctx, axis=1, keepdims=True)), snk)
        p_loc = jnp.exp(s_loc - m)
        p_ctx = jnp.exp(s_ctx - m)
        l = (jnp.sum(p_loc, axis=1, keepdims=True) + jnp.sum(p_ctx, axis=1, keepdims=True)
             + jnp.exp(snk - m))
        o_ref[:, h * 64:(h + 1) * 64] = (_dot(p_loc, vw[:, ksl]) + _dot(p_ctx, cv_ref[:, ksl])) / l


def _lat_swa(p, sink, ck, cv, cos, sin):
    qb0 = N_CTX // SW_BLOCK
    kb0 = N_CTX // DEC_SEQ
    return pl.pallas_call(
        _lat_swa_kernel,
        out_shape=jax.ShapeDtypeStruct((N_LAT, BRANCH_W), F32),
        grid=(DEC_BATCH, SW_NB),
        in_specs=[
            pl.BlockSpec(memory_space=pltpu.SMEM),
            pl.BlockSpec((SW_BLOCK, 256), lambda b, n: (qb0 + b * SW_NB + n, P_SWQ // 256)),
            pl.BlockSpec((DEC_SEQ, 128), lambda b, n: (kb0 + b, P_SWK // 128)),
            pl.BlockSpec((DEC_SEQ, 128), lambda b, n: (kb0 + b, P_SWV // 128)),
            pl.BlockSpec((None, PAST_LEN, 128), lambda b, n: (b, 0, 0)),
            pl.BlockSpec((None, PAST_LEN, 128), lambda b, n: (b, 0, 0)),
            pl.BlockSpec((DEC_SEQ, 256), lambda b, n: (0, 0)),
            pl.BlockSpec((DEC_SEQ, 256), lambda b, n: (0, 0)),
        ],
        out_specs=pl.BlockSpec((SW_BLOCK, BRANCH_W), lambda b, n: (b * SW_NB + n, 0)),
        compiler_params=_cparams(("parallel", "arbitrary")),
        name="lat_swa",
    )(sink, p, p, p, ck, cv, cos, sin)


def _rope_tables():
    t = np.arange(DEC_SEQ)
    half = 16
    freqs = ROPE_BASE ** (-np.arange(half, dtype=np.float64) / half)
    ang_r = (t // GRID_W)[:, None] * freqs[None, :]
    ang_c = (t % GRID_W)[:, None] * freqs[None, :]
    cos = np.concatenate([np.cos(ang_r), np.cos(ang_r), np.cos(ang_c), np.cos(ang_c)], axis=1)
    sin = np.concatenate([-np.sin(ang_r), np.sin(ang_r), -np.sin(ang_c), np.sin(ang_c)], axis=1)
    return (jnp.asarray(np.tile(cos, (1, 4)), F32), jnp.asarray(np.tile(sin, (1, 4)), F32))


def _dir_masks(d, n):
    row = lax.broadcasted_iota(jnp.int32, (n, n), 0)
    col = lax.broadcasted_iota(jnp.int32, (n, n), 1)
    u = (col - row) * (1 - 2 * d)
    return row, col, u


def _seq_specs(nblk, b0_blk):
    def blk(b, d, j):
        return b0_blk + b * nblk + jnp.where(d == 0, j, nblk - 1 - j)
    return blk


def _mlstm_kernel(qkv_ref, gt_ref, gc_ref, c0_ref, m0_ref, br_ref, bc_ref,
                  h_ref, cfin_ref, mfin_ref, c_s, m_s, *, nblk):
    n = SEQ_BLK
    d = pl.program_id(1)
    j = pl.program_id(2)

    @pl.when(j == 0)
    def _():
        c_s[...] = c0_ref[...]
        m_s[...] = m0_ref[...]

    _, col, u = _dir_masks(d, n)
    incl = u <= 0
    tri_c = jnp.where(incl, 1.0, 0.0).astype(F32)
    tri_r = jnp.where(u >= 0, 1.0, 0.0).astype(F32)
    end_lane = jnp.where(d == 0, n - 1, 0)

    g_r = gt_ref[0:8, :] + br_ref[...]
    f_r = _log_sigmoid(g_r)
    bcum_r = _dot_f32(f_r, tri_r)
    g_c = gc_ref[...] + bc_ref[...]
    f_c = _log_sigmoid(g_c)
    bcum_c = _dot_f32(tri_c, f_c)

    lane_r = lax.broadcasted_iota(jnp.int32, (1, n), 1)
    ones_col = jnp.where(lax.broadcasted_iota(jnp.int32, (n, 64), 1) == 0, 1.0, 0.0).astype(F32)

    for h in range(ML_HEADS):
        q = qkv_ref[:, h * 64:(h + 1) * 64]
        k = qkv_ref[:, 256 + h * 64:256 + (h + 1) * 64] * (64 ** -0.5)
        v = qkv_ref[:, 512 + h * 64:512 + (h + 1) * 64]
        i_r = g_r[h:h + 1, :]
        b_r = bcum_r[4 + h:5 + h, :]
        i_c = g_c[:, h:h + 1]
        b_c = bcum_c[:, 4 + h:5 + h]
        m_prev = m_s[h][:, 0:1]
        cp = c_s[h]

        r_row = i_r - b_r
        dmat = jnp.where(incl, b_c + r_row, -jnp.inf)
        inter = b_c + m_prev
        mt = jnp.maximum(inter, jnp.max(dmat, axis=1, keepdims=True))
        wd = jnp.exp(dmat - mt)
        wi = jnp.exp(inter - mt)
        s = _dot_nt(q, k) * wd
        vp = jnp.concatenate([v, ones_col], axis=1)
        numden = _dot(s, vp) + wi * _dot(q, cp)
        num = numden[:, 0:64]
        den = numden[:, 64:65]
        h_ref[:, h * 64:(h + 1) * 64] = num / jnp.maximum(jnp.abs(den), jnp.exp(-mt))

        b_last = jnp.sum(jnp.where(lane_r == end_lane, b_r, 0.0), axis=1, keepdims=True)
        gw_r = b_last + r_row
        m_new = jnp.maximum(b_last + m_prev, jnp.max(gw_r, axis=1, keepdims=True))
        wc = jnp.exp(b_last + m_prev - m_new)
        wg_c = jnp.exp(b_last + (i_c - b_c) - m_new)
        c_s[h] = wc * cp + _dot_tn(k * wg_c, vp)
        m_s[h] = jnp.broadcast_to(m_new, (1, 128))

    @pl.when(j == nblk - 1)
    def _():
        cfin_ref[...] = c_s[...]
        mfin_ref[...] = m_s[...]


def _mlstm(p, gt, c0p, m0p, br, bc, nb, t, tok0):
    nblk = t // SEQ_BLK
    blk = _seq_specs(nblk, tok0 // SEQ_BLK)
    blk_out = _seq_specs(nblk, 0)
    ntok = nb * t
    return pl.pallas_call(
        functools.partial(_mlstm_kernel, nblk=nblk),
        out_shape=(jax.ShapeDtypeStruct((2, ntok, 256), F32),
                   jax.ShapeDtypeStruct((nb, 2, ML_HEADS, 64, 128), F32),
                   jax.ShapeDtypeStruct((nb, 2, ML_HEADS, 1, 128), F32)),
        grid=(nb, 2, nblk),
        in_specs=[
            pl.BlockSpec((SEQ_BLK, 768), lambda b, d, j: (blk(b, d, j), P_ML // 768)),
            pl.BlockSpec((None, 16, SEQ_BLK), lambda b, d, j: (d, 0, blk(b, d, j))),
            pl.BlockSpec((SEQ_BLK, 128), lambda b, d, j: (blk(b, d, j), P_GATE // 128 + d)),
            pl.BlockSpec((None, None, ML_HEADS, 64, 128), lambda b, d, j: (b, d, 0, 0, 0)),
            pl.BlockSpec((None, None, ML_HEADS, 1, 128), lambda b, d, j: (b, d, 0, 0, 0)),
            pl.BlockSpec((None, 8, 1), lambda b, d, j: (d, 0, 0)),
            pl.BlockSpec((None, 1, 128), lambda b, d, j: (d, 0, 0)),
        ],
        out_specs=(pl.BlockSpec((None, SEQ_BLK, 256), lambda b, d, j: (d, blk_out(b, d, j), 0)),
                   pl.BlockSpec((None, None, ML_HEADS, 64, 128), lambda b, d, j: (b, d, 0, 0, 0)),
                   pl.BlockSpec((None, None, ML_HEADS, 1, 128), lambda b, d, j: (b, d, 0, 0, 0))),
        scratch_shapes=[pltpu.VMEM((ML_HEADS, 64, 128), F32), pltpu.VMEM((ML_HEADS, 1, 128), F32)],
        compiler_params=_cparams(("parallel", "arbitrary", "arbitrary")),
        name="mlstm",
    )(p, gt, p, c0p, m0p, br, bc)


def _gdn_pre_kernel(x_ref, w_ref, o_ref, pad_s, *, t):
    pad_s[0:8, :] = jnp.zeros((8, 768), F32)
    pad_s[t + 8:t + 16, :] = jnp.zeros((8, 768), F32)
    pad_s[8:t + 8, :] = x_ref[...]
    y = jnp.zeros((t, 768), F32)
    for jj in range(DN_CONV):
        off = 8 + jj - DN_CONV // 2
        y = y + pad_s[off:off + t, :] * w_ref[jj:jj + 1, :]
    a = y * _sigmoid(y)
    for h in range(DN_HEADS):
        for part, mul in ((0, 64 ** -0.5), (256, 1.0)):
            sl = slice(part + h * 64, part + (h + 1) * 64)
            z = a[:, sl]
            o_ref[:, sl] = z * lax.rsqrt(jnp.sum(z * z, axis=1, keepdims=True) + NORM_EPS) * mul
    o_ref[:, 512:768] = a[:, 512:768]


def _gdn_pre(p, conv_w, nb, t, tok0):
    b0 = tok0 // t
    return pl.pallas_call(
        functools.partial(_gdn_pre_kernel, t=t),
        out_shape=jax.ShapeDtypeStruct((nb * t, 768), F32),
        grid=(nb,),
        in_specs=[
            pl.BlockSpec((t, 768), lambda b: (b0 + b, P_DN // 768)),
            pl.BlockSpec((8, 768), lambda b: (0, 0)),
        ],
        out_specs=pl.BlockSpec((t, 768), lambda b: (b, 0)),
        scratch_shapes=[pltpu.VMEM((t + 16, 768), F32)],
        compiler_params=_cparams(("parallel",)),
        name="gdn_pre",
    )(p, conv_w)


def _gdn_kernel(qkv_ref, gt_ref, gc_ref, s0_ref, pr_ref, pc_ref, o_ref, sfin_ref, s_s, *, nblk):
    n = SEQ_BLK
    d = pl.program_id(1)
    j = pl.program_id(2)

    @pl.when(j == 0)
    def _():
        s_s[...] = s0_ref[...]

    row, col, u = _dir_masks(d, n)
    incl = u <= 0
    strict = u < 0
    tri_c = jnp.where(incl, 1.0, 0.0).astype(F32)
    tri_r = jnp.where(u >= 0, 1.0, 0.0).astype(F32)
    end_lane = jnp.where(d == 0, n - 1, 0)
    eye = jnp.where(u == 0, 1.0, 0.0).astype(F32)

    a_r = gt_ref[8:16, :]
    g_r = -jnp.exp(pr_ref[0:8, :]) * _softplus(a_r + pr_ref[8:16, :])
    gcum_r = _dot_f32(g_r, tri_r)
    a_c = gc_ref[...]
    g_c = -jnp.exp(pc_ref[1:2, :]) * _softplus(a_c + pc_ref[0:1, :])
    gcum_c = _dot_f32(tri_c, g_c)
    beta_c = _sigmoid(a_c)
    lane_r = lax.broadcasted_iota(jnp.int32, (1, n), 1)

    for h in range(DN_HEADS):
        qn = qkv_ref[:, h * 64:(h + 1) * 64]
        kn = qkv_ref[:, 256 + h * 64:256 + (h + 1) * 64]
        v = qkv_ref[:, 512 + h * 64:512 + (h + 1) * 64]
        gc_row = gcum_r[h:h + 1, :]
        gc_col = gcum_c[:, 8 + h:9 + h]
        beta = beta_c[:, 12 + h:13 + h]
        s_prev = s_s[h]

        decay = jnp.exp(jnp.where(incl, gc_col - gc_row, -jnp.inf))
        kb = kn * beta
        a_mat = jnp.where(strict, _dot_nt(kb, kn) * decay, 0.0)
        e_col = jnp.exp(gc_col)
        rhs = jnp.concatenate([v * beta, kb * e_col], axis=1)

        same2 = (row >> 1) == (col >> 1)
        tinv = eye - jnp.where(same2, a_mat, 0.0)
        for lb in range(1, 8):
            inner = (row >> lb) == (col >> lb)
            outer = (row >> (lb + 1)) == (col >> (lb + 1))
            a_off = jnp.where(jnp.logical_and(outer, jnp.logical_not(inner)), a_mat, 0.0)
            tinv = tinv - _dot(tinv, _dot(a_off, tinv))
        sol = _dot(tinv, rhs)
        uu = sol[:, 0:64]
        ww = sol[:, 64:128]

        qk = _dot_nt(qn, kn) * decay
        g_last = jnp.sum(jnp.where(lane_r == end_lane, gc_row, 0.0), axis=1, keepdims=True)
        q_dec = qn * e_col
        k_dec = kn * jnp.exp(g_last - gc_col)
        v_new = uu - _dot(ww, s_prev)
        o_ref[:, h * 64:(h + 1) * 64] = _dot(q_dec, s_prev) + _dot(qk, v_new)
        s_s[h] = jnp.exp(g_last) * s_prev + _dot_tn(k_dec, v_new)

    @pl.when(j == nblk - 1)
    def _():
        sfin_ref[...] = s_s[...]


def _gdn(qkv, p, gt, s0, pr, pc, nb, t, tok0):
    nblk = t // SEQ_BLK
    blk = _seq_specs(nblk, tok0 // SEQ_BLK)
    blk0 = _seq_specs(nblk, 0)
    ntok = nb * t
    return pl.pallas_call(
        functools.partial(_gdn_kernel, nblk=nblk),
        out_shape=(jax.ShapeDtypeStruct((2, ntok, 256), F32),
                   jax.ShapeDtypeStruct((nb, 2, DN_HEADS, 64, 64), F32)),
        grid=(nb, 2, nblk),
        in_specs=[
            pl.BlockSpec((SEQ_BLK, 768), lambda b, d, j: (blk0(b, d, j), 0)),
            pl.BlockSpec((None, 16, SEQ_BLK), lambda b, d, j: (d, 0, blk(b, d, j))),
            pl.BlockSpec((SEQ_BLK, 128), lambda b, d, j: (blk(b, d, j), P_GATE // 128 + d)),
            pl.BlockSpec((None, None, DN_HEADS, 64, 64), lambda b, d, j: (b, d, 0, 0, 0)),
            pl.BlockSpec((None, 16, 1), lambda b, d, j: (d, 0, 0)),
            pl.BlockSpec((None, 2, 128), lambda b, d, j: (d, 0, 0)),
        ],
        out_specs=(pl.BlockSpec((None, SEQ_BLK, 256), lambda b, d, j: (d, blk0(b, d, j), 0)),
                   pl.BlockSpec((None, None, DN_HEADS, 64, 64), lambda b, d, j: (b, d, 0, 0, 0))),
        scratch_shapes=[pltpu.VMEM((DN_HEADS, 64, 64), F32)],
        compiler_params=_cparams(("parallel", "arbitrary", "arbitrary")),
        name="gdn",
    )(qkv, gt, p, s0, pr, pc)


MG_TM = 512


def _merge_kernel(x_ref, sh_ref, sc_ref, gt_ref, ona_ref, osw_ref, hf_ref, hb_ref, mlo_ref,
                  df_ref, db_ref, dng_ref, mlg_ref, dngain_ref, wmg_ref, wb_ref, wo_ref,
                  lng_ref, lnb_ref, o_ref):
    x = x_ref[...]
    xm = (x * (1.0 + sc_ref[...]) + sh_ref[...]).astype(BF16)
    hh = hf_ref[...] + hb_ref[...]
    dd = df_ref[...] + db_ref[...]
    ml_parts = []
    dn_parts = []
    for h in range(ML_HEADS):
        sl = slice(h * 64, (h + 1) * 64)
        z = hh[:, sl]
        mu = jnp.mean(z, axis=1, keepdims=True)
        zc = z - mu
        var = jnp.mean(zc * zc, axis=1, keepdims=True)
        ml_parts.append(zc * lax.rsqrt(var + LN_EPS))
        z = dd[:, sl]
        dn_parts.append(z * lax.rsqrt(jnp.mean(z * z, axis=1, keepdims=True) + NORM_EPS))
    o_ml = jnp.concatenate(ml_parts, axis=1) * mlg_ref[...] * _sigmoid(mlo_ref[...])
    g_pre = dng_ref[...]
    o_dn = jnp.concatenate(dn_parts, axis=1) * dngain_ref[...] * (g_pre * _sigmoid(g_pre))
    acc = jnp.zeros((x.shape[0], D_MODEL), F32)
    for nbr, o_n in enumerate((ona_ref[...], o_ml, o_dn, osw_ref[...])):
        y = _dot(o_n, wb_ref[nbr])
        gate = _sigmoid(jnp.dot(xm, wmg_ref[:, nbr * D_MODEL:(nbr + 1) * D_MODEL],
                                preferred_element_type=F32))
        acc = acc + gate * y
    mix = _dot(acc, wo_ref[...])
    o_ref[...] = _layer_norm(ALPHA * x + gt_ref[...] * mix, lng_ref[...], lnb_ref[...])


def _merge(x, mod_l, p, o_na, o_sw, hml, odn, ml_gain, dn_gain, wmg, wb, wo, ln_g, ln_b):
    tm = MG_TM
    row = lambda i: (i, 0)
    const2 = lambda i: (0, 0)
    return pl.pallas_call(
        _merge_kernel,
        out_shape=jax.ShapeDtypeStruct((N_TOK, D_MODEL), F32),
        grid=(N_TOK // tm,),
        in_specs=[
            pl.BlockSpec((tm, D_MODEL), row),
            _mod_spec(0, tm), _mod_spec(1, tm), _mod_spec(2, tm),
            pl.BlockSpec((tm, 256), row),
            pl.BlockSpec((tm, 256), row),
            pl.BlockSpec((None, tm, 256), lambda i: (0, i, 0)),
            pl.BlockSpec((None, tm, 256), lambda i: (1, i, 0)),
            pl.BlockSpec((tm, 256), lambda i: (i, P_MLO // 256)),
            pl.BlockSpec((None, tm, 256), lambda i: (0, i, 0)),
            pl.BlockSpec((None, tm, 256), lambda i: (1, i, 0)),
            pl.BlockSpec((tm, 256), lambda i: (i, P_DNG // 256)),
            pl.BlockSpec((1, 256), const2),
            pl.BlockSpec((1, 256), const2),
            pl.BlockSpec((D_MODEL, N_BRANCH * D_MODEL), const2),
            pl.BlockSpec((N_BRANCH, BRANCH_W, D_MODEL), lambda i: (0, 0, 0)),
            pl.BlockSpec((D_MODEL, D_MODEL), const2),
            pl.BlockSpec((1, D_MODEL), const2),
            pl.BlockSpec((1, D_MODEL), const2),
        ],
        out_specs=pl.BlockSpec((tm, D_MODEL), row),
        compiler_params=_cparams(("parallel",)),
        name="merge",
    )(x, mod_l, mod_l, mod_l, o_na, o_sw, hml, hml, p, odn, odn, p, ml_gain, dn_gain,
      wmg, wb, wo, ln_g, ln_b)


FF_TM = 512
FF_TH = 1408
FF_NK = FFN_DIM // FF_TH


def _ffn_kernel(x_ref, sh_ref, sc_ref, gt_ref, wg_ref, wu_ref, wo_ref, lng_ref, lnb_ref, o_ref,
                xm_s, acc_s):
    k = pl.program_id(1)

    @pl.when(k == 0)
    def _():
        xm_s[...] = (x_ref[...] * (1.0 + sc_ref[...]) + sh_ref[...]).astype(BF16)
        acc_s[...] = jnp.zeros_like(acc_s)

    xm = xm_s[...]
    hg = jnp.dot(xm, wg_ref[...], preferred_element_type=F32)
    hu = jnp.dot(xm, wu_ref[...], preferred_element_type=F32)
    a = hg * _sigmoid(hg) * hu
    acc_s[...] += _dot(a, wo_ref[...])

    @pl.when(k == FF_NK - 1)
    def _():
        z = ALPHA * x_ref[...] + gt_ref[...] * acc_s[...]
        o_ref[...] = _layer_norm(z, lng_ref[...], lnb_ref[...])


def _ffn(x, mod_l, w_in, w_out, ln_g, ln_b):
    tm = FF_TM
    return pl.pallas_call(
        _ffn_kernel,
        out_shape=jax.ShapeDtypeStruct((N_TOK, D_MODEL), F32),
        grid=(N_TOK // tm, FF_NK),
        in_specs=[
            pl.BlockSpec((tm, D_MODEL), lambda i, k: (i, 0)),
            _mod_spec(3, tm), _mod_spec(4, tm), _mod_spec(5, tm),
            pl.BlockSpec((D_MODEL, FF_TH), lambda i, k: (0, k)),
            pl.BlockSpec((D_MODEL, FF_TH), lambda i, k: (0, FF_NK + k)),
            pl.BlockSpec((FF_TH, D_MODEL), lambda i, k: (k, 0)),
            pl.BlockSpec((1, D_MODEL), lambda i, k: (0, 0)),
            pl.BlockSpec((1, D_MODEL), lambda i, k: (0, 0)),
        ],
        out_specs=pl.BlockSpec((tm, D_MODEL), lambda i, k: (i, 0)),
        scratch_shapes=[pltpu.VMEM((tm, D_MODEL), BF16), pltpu.VMEM((tm, D_MODEL), F32)],
        compiler_params=_cparams(("parallel", "arbitrary")),
        name="ffn",
    )(x, mod_l, mod_l, mod_l, w_in, w_in, w_out, ln_g, ln_b)


def _router_kernel(x_ref, sh_ref, sc_ref, wr_ref, br_ref, g_ref):
    xm = x_ref[...] * (1.0 + sc_ref[...]) + sh_ref[...]
    logits = _dot_f32(xm, wr_ref[...]) + br_ref[...]
    lane = lax.broadcasted_iota(jnp.int32, logits.shape, 1)
    neg = jnp.where(lane < N_EXPERTS, logits, -jnp.inf)
    v1 = jnp.max(neg, axis=1, keepdims=True)
    i1 = jnp.min(jnp.where(neg == v1, lane, 128), axis=1, keepdims=True)
    rest = jnp.where(lane == i1, -jnp.inf, neg)
    v2 = jnp.max(rest, axis=1, keepdims=True)
    i2 = jnp.min(jnp.where(rest == v2, lane, 128), axis=1, keepdims=True)
    e2 = jnp.exp(v2 - v1)
    p1 = 1.0 / (1.0 + e2)
    p2 = e2 / (1.0 + e2)
    g_ref[...] = jnp.where(lane == i1, p1, 0.0) + jnp.where(lane == i2, p2, 0.0)


def _router(x, mod_l, wr, br):
    tm = 512
    return pl.pallas_call(
        _router_kernel,
        out_shape=jax.ShapeDtypeStruct((N_TOK, 128), F32),
        grid=(N_TOK // tm,),
        in_specs=[
            pl.BlockSpec((tm, D_MODEL), lambda i: (i, 0)),
            _mod_spec(3, tm), _mod_spec(4, tm),
            pl.BlockSpec((D_MODEL, 128), lambda i: (0, 0)),
            pl.BlockSpec((1, 128), lambda i: (0, 0)),
        ],
        out_specs=pl.BlockSpec((tm, 128), lambda i: (i, 0)),
        compiler_params=_cparams(("parallel",)),
        name="router",
    )(x, mod_l, mod_l, wr, br)


def _moe_kernel(x_ref, sh_ref, sc_ref, gt_ref, gates_ref, wg_ref, wu_ref, wo_ref, lng_ref, lnb_ref,
                o_ref, xm_s, acc_s):
    e = pl.program_id(1)
    k = pl.program_id(2)

    @pl.when(jnp.logical_and(e == 0, k == 0))
    def _():
        xm_s[...] = (x_ref[...] * (1.0 + sc_ref[...]) + sh_ref[...]).astype(BF16)
        acc_s[...] = jnp.zeros_like(acc_s)

    xm = xm_s[...]
    hg = jnp.dot(xm, wg_ref[...], preferred_element_type=F32)
    hu = jnp.dot(xm, wu_ref[...], preferred_element_type=F32)
    a = hg * _sigmoid(hg) * hu
    lane = lax.broadcasted_iota(jnp.int32, gates_ref.shape, 1)
    gate = jnp.sum(jnp.where(lane == e, gates_ref[...], 0.0), axis=1, keepdims=True)
    acc_s[...] += gate * _dot(a, wo_ref[...])

    @pl.when(jnp.logical_and(e == N_EXPERTS - 1, k == FF_NK - 1))
    def _():
        z = ALPHA * x_ref[...] + gt_ref[...] * acc_s[...]
        o_ref[...] = _layer_norm(z, lng_ref[...], lnb_ref[...])


def _moe(x, mod_l, gates, w_in, w_out, ln_g, ln_b):
    tm = FF_TM
    return pl.pallas_call(
        _moe_kernel,
        out_shape=jax.ShapeDtypeStruct((N_TOK, D_MODEL), F32),
        grid=(N_TOK // tm, N_EXPERTS, FF_NK),
        in_specs=[
            pl.BlockSpec((tm, D_MODEL), lambda i, e, k: (i, 0)),
            _mod_spec(3, tm), _mod_spec(4, tm), _mod_spec(5, tm),
            pl.BlockSpec((tm, 128), lambda i, e, k: (i, 0)),
            pl.BlockSpec((None, D_MODEL, FF_TH), lambda i, e, k: (e, 0, k)),
            pl.BlockSpec((None, D_MODEL, FF_TH), lambda i, e, k: (e, 0, FF_NK + k)),
            pl.BlockSpec((None, FF_TH, D_MODEL), lambda i, e, k: (e, k, 0)),
            pl.BlockSpec((1, D_MODEL), lambda i, e, k: (0, 0)),
            pl.BlockSpec((1, D_MODEL), lambda i, e, k: (0, 0)),
        ],
        out_specs=pl.BlockSpec((tm, D_MODEL), lambda i, e, k: (i, 0)),
        scratch_shapes=[pltpu.VMEM((tm, D_MODEL), BF16), pltpu.VMEM((tm, D_MODEL), F32)],
        compiler_params=_cparams(("parallel", "arbitrary", "arbitrary")),
        name="moe",
    )(x, mod_l, mod_l, mod_l, gates, w_in, w_in, w_out, ln_g, ln_b)


def _split_w_in(w):
    na = w[:, 0:768]
    ml = w[:, 768:1536]
    mlg = w[:, 1536:1552]
    mlo = w[:, 1552:1808]
    dn = w[:, 1808:2576]
    dna = w[:, 2576:2584]
    dnb = w[:, 2584:2592]
    dng = w[:, 2592:2848]
    sw = w[:, 2848:3360]
    mg = w[:, 3360:7456]
    zpad = jnp.zeros((D_MODEL, 112), w.dtype)
    gates = []
    for d in range(2):
        gates.append(jnp.concatenate([mlg[:, 4 * d:4 * d + 4], mlg[:, 8 + 4 * d:12 + 4 * d],
                                      dna[:, 4 * d:4 * d + 4], dnb[:, 4 * d:4 * d + 4]], axis=1))
    w1 = jnp.concatenate([na, ml, dn, mlo, dng, sw, gates[0], zpad, gates[1], zpad], axis=1)
    wt = jnp.concatenate(gates, axis=1).T
    return w1.astype(BF16), wt.astype(BF16), mg.astype(BF16)


def kernel(x_prompt, x_sample, c, cache_na_k, cache_na_v, cache_sw_k, cache_sw_v, state_ml_c, state_ml_n, state_ml_m, state_dn_s, c_ctx, ada_w, ada_b, w_in, na_rpb, ml_gate_b, ml_norm_g, dn_conv_w, dn_a_log, dn_dt_bias, dn_norm_g, sw_sink, w_branch, w_out, ln_g, ln_b, ffn_w_in, ffn_w_out, moe_router, moe_router_b, moe_w_in, moe_w_out):
    x = jnp.concatenate([x_prompt.reshape(N_CTX, D_MODEL), x_sample.reshape(N_LAT, D_MODEL)], axis=0)
    cvec = jnp.concatenate([c_ctx[None, :], c, jnp.zeros((3, D_MODEL), F32)], axis=0)
    mod = _ada(cvec, ada_w, ada_b).reshape(DEPTH, 8, 6, 1, D_MODEL)
    cos, sin = _rope_tables()

    zc_ctx = jnp.zeros((BATCH, 2, ML_HEADS, 64, 128), F32)
    zm_ctx = jnp.zeros((BATCH, 2, ML_HEADS, 1, 128), F32)
    zs_ctx = jnp.zeros((BATCH, 2, DN_HEADS, 64, 64), F32)
    pad63 = jnp.zeros((DEC_BATCH, 2, ML_HEADS, 64, 63), F32)

    new = [[] for _ in range(8)]
    for l in range(DEPTH):
        w1, wt, wmg = _split_w_in(w_in[l])
        mod_l = mod[l]
        p, gt = _inproj(x, mod_l, w1, wt)

        o_na_c, o_sw_c = _ctx_attn(p, sw_sink[l])
        ck = cache_na_k[:, l].reshape(DEC_BATCH, PAST_LEN, 256)
        cv = cache_na_v[:, l].reshape(DEC_BATCH, PAST_LEN, 256)
        o_na_l = _lat_na(p, ck, cv, _na_bias_table(na_rpb[l]))
        sk = cache_sw_k[:, l].reshape(DEC_BATCH, PAST_LEN, 128)
        sv = cache_sw_v[:, l].reshape(DEC_BATCH, PAST_LEN, 128)
        o_sw_l = _lat_swa(p, sw_sink[l], sk, sv, cos, sin)

        gb = ml_gate_b[l]
        br = jnp.stack([jnp.concatenate([gb[0], gb[2]]), jnp.concatenate([gb[1], gb[3]])])[:, :, None]
        bc = jnp.pad(br[:, :, 0], ((0, 0), (0, 120)))[:, None, :]
        h_c, cfin, mfin = _mlstm(p, gt, zc_ctx, zm_ctx, br, bc, BATCH, SEQ, 0)
        c0p = jnp.concatenate([state_ml_c[:, l], state_ml_n[:, l][..., None], pad63], axis=-1)
        m0p = jnp.broadcast_to(state_ml_m[:, l][..., None, None], (DEC_BATCH, 2, ML_HEADS, 1, 128))
        h_l, _, _ = _mlstm(p, gt, c0p, m0p, br, bc, DEC_BATCH, DEC_SEQ, N_CTX)

        conv_w = jnp.pad(dn_conv_w[l], ((0, 3), (0, 0)))
        z4 = jnp.zeros((2, 4), F32)
        pr = jnp.concatenate([dn_a_log[l], z4, dn_dt_bias[l], z4], axis=1)[:, :, None]
        pc = jnp.stack([jnp.pad(dn_dt_bias[l], ((0, 0), (8, 116))),
                        jnp.pad(dn_a_log[l], ((0, 0), (8, 116)))], axis=1)
        qkv_c = _gdn_pre(p, conv_w, BATCH, SEQ, 0)
        o_c, sfin = _gdn(qkv_c, p, gt, zs_ctx, pr, pc, BATCH, SEQ, 0)
        qkv_l = _gdn_pre(p, conv_w, DEC_BATCH, DEC_SEQ, N_CTX)
        o_l, _ = _gdn(qkv_l, p, gt, state_dn_s[:, l], pr, pc, DEC_BATCH, DEC_SEQ, N_CTX)

        o_na = jnp.concatenate([o_na_c, o_na_l], axis=0)
        o_sw = jnp.concatenate([o_sw_c, o_sw_l], axis=0)
        hml = jnp.concatenate([h_c, h_l], axis=1)
        odn = jnp.concatenate([o_c, o_l], axis=1)
        x = _merge(x, mod_l, p, o_na, o_sw, hml, odn, ml_norm_g[l][None, :],
                   jnp.tile(dn_norm_g[l], DN_HEADS)[None, :], wmg, w_branch[l].astype(BF16),
                   w_out[l].astype(BF16), ln_g[l, 0][None, :], ln_b[l, 0][None, :])

        jx = l // 2
        if l % 2 == 0:
            x = _ffn(x, mod_l, ffn_w_in[jx].astype(BF16), ffn_w_out[jx].astype(BF16),
                     ln_g[l, 1][None, :], ln_b[l, 1][None, :])
        else:
            wr = jnp.pad(moe_router[jx], ((0, 0), (0, 120)))
            brt = jnp.pad(moe_router_b[jx], (0, 120))[None, :]
            gates = _router(x, mod_l, wr, brt)
            x = _moe(x, mod_l, gates, moe_w_in[jx].astype(BF16), moe_w_out[jx].astype(BF16),
                     ln_g[l, 1][None, :], ln_b[l, 1][None, :])

        pc_ = p[0:N_CTX]
        new[0].append(pc_[:, 256:512].reshape(BATCH, SEQ, NA_HEADS, HEAD_DIM))
        new[1].append(pc_[:, 512:768].reshape(BATCH, SEQ, NA_HEADS, HEAD_DIM))
        new[2].append(pc_[:, P_SWK:P_SWK + 128].reshape(BATCH, SEQ, SW_KV_HEADS, HEAD_DIM))
        new[3].append(pc_[:, P_SWV:P_SWV + 128].reshape(BATCH, SEQ, SW_KV_HEADS, HEAD_DIM))
        new[4].append(cfin[..., 0:64])
        new[5].append(cfin[..., 64])
        new[6].append(mfin[..., 0, 0])
        new[7].append(sfin)

    y_prompt = x[0:N_CTX].reshape(BATCH, SEQ, D_MODEL)
    y_sample = x[N_CTX:].reshape(DEC_BATCH, DEC_SEQ, D_MODEL)
    outs = [jnp.stack(s, axis=1) for s in new]
    return (y_prompt, y_sample) + tuple(outs)
```

```python
import functools
import math

import numpy as np
import jax
import jax.numpy as jnp
from jax import lax
from jax.experimental import pallas as pl
from jax.experimental.pallas import tpu as pltpu

F32 = jnp.float32
BF16 = jnp.bfloat16
HIGHEST = lax.Precision.HIGHEST

D_MODEL = 1024
BATCH = 16
SEQ = 256
DEPTH = 4
DEC_BATCH = 4
DEC_SEQ = 1024
PAST_LEN = 512
GRID_W = 64
HEAD_DIM = 64
NA_HEADS = 4
NA_WR = 8
NA_WC = 16
ML_HEADS = 4
DN_HEADS = 4
DN_CONV = 5
SW_HEADS = 4
SW_KV_HEADS = 2
SW_WINDOW = 128
SW_BLOCK = 128
ROPE_BASE = 10000.0
N_BRANCH = 4
FFN_DIM = 2816
N_EXPERTS = 8
ALPHA = (2 * DEPTH) ** 0.25
LN_EPS = 1e-5
NORM_EPS = 1e-6

N_CTX = BATCH * SEQ
N_LAT = DEC_BATCH * DEC_SEQ
N_TOK = N_CTX + N_LAT
BRANCH_W = NA_HEADS * HEAD_DIM

P_NA = 0
P_ML = 768
P_DN = 1536
P_MLO = 2304
P_DNG = 2560
P_SWQ = 2816
P_SWK = 3072
P_SWV = 3200
P_GATE = 3328
P_W = 3584

SEQ_BLK = 256
V7X_VMEM_LIMIT = 56 * 1024 * 1024


def _cparams(sem, vmem=V7X_VMEM_LIMIT):
    return pltpu.CompilerParams(dimension_semantics=sem, vmem_limit_bytes=vmem)


def _dot(a, b):
    return jnp.dot(a.astype(BF16), b.astype(BF16), preferred_element_type=F32)


def _dot_nt(a, b):
    return lax.dot_general(a.astype(BF16), b.astype(BF16), (((1,), (1,)), ((), ())),
                           preferred_element_type=F32)


def _dot_tn(a, b):
    return lax.dot_general(a.astype(BF16), b.astype(BF16), (((0,), (0,)), ((), ())),
                           preferred_element_type=F32)


def _dot_f32(a, b):
    return jnp.dot(a, b, precision=HIGHEST, preferred_element_type=F32)


def _sigmoid(x):
    return 1.0 / (1.0 + jnp.exp(-x))


def _softplus(x):
    return jnp.maximum(x, 0.0) + jnp.log(1.0 + jnp.exp(-jnp.abs(x)))


def _log_sigmoid(x):
    return -_softplus(-x)


def _mod_row(tok_start):
    return jnp.where(tok_start < N_CTX, 0, 1 + (tok_start - N_CTX) // DEC_SEQ)


def _layer_norm(z, g, b):
    mu = jnp.mean(z, axis=-1, keepdims=True)
    zc = z - mu
    var = jnp.mean(zc * zc, axis=-1, keepdims=True)
    return zc * lax.rsqrt(var + LN_EPS) * g + b


def _ada_kernel(c_ref, w_ref, b_ref, o_ref):
    c = c_ref[...]
    s = c * _sigmoid(c)
    o_ref[...] = _dot(s, w_ref[...]) + b_ref[...]


def _ada(cvec, ada_w, ada_b):
    tn = 1536
    n = 6 * D_MODEL
    return pl.pallas_call(
        _ada_kernel,
        out_shape=jax.ShapeDtypeStruct((DEPTH, 8, n), F32),
        grid=(DEPTH, n // tn),
        in_specs=[
            pl.BlockSpec((8, D_MODEL), lambda l, j: (0, 0)),
            pl.BlockSpec((None, D_MODEL, tn), lambda l, j: (l, 0, j)),
            pl.BlockSpec((None, 1, tn), lambda l, j: (l, 0, j)),
        ],
        out_specs=pl.BlockSpec((None, 8, tn), lambda l, j: (l, 0, j)),
        compiler_params=_cparams(("parallel", "parallel")),
        name="ada",
    )(cvec, ada_w, ada_b.reshape(DEPTH, 1, n))


def _mod_spec(chunk, tm):
    return pl.BlockSpec((None, None, 1, D_MODEL), lambda i, *_: (_mod_row(i * tm), chunk, 0, 0))


IN_TM = 512
IN_NCH = 4


def _inproj_kernel(x_ref, sh_ref, sc_ref, w_ref, wt_ref, p_ref, gt_ref):
    xm = (x_ref[...] * (1.0 + sc_ref[...]) + sh_ref[...]).astype(BF16)
    cw = P_W // IN_NCH
    for c in range(IN_NCH):
        p_ref[:, c * cw:(c + 1) * cw] = jnp.dot(xm, w_ref[:, c * cw:(c + 1) * cw],
                                                preferred_element_type=F32)
    gt = lax.dot_general(wt_ref[...], xm, (((1,), (1,)), ((), ())), preferred_element_type=F32)
    gt_ref[0] = gt[0:16]
    gt_ref[1] = gt[16:32]


def _inproj(x, mod_l, w1, wt):
    tm = IN_TM
    return pl.pallas_call(
        _inproj_kernel,
        out_shape=(jax.ShapeDtypeStruct((N_TOK, P_W), F32),
                   jax.ShapeDtypeStruct((2, 16, N_TOK), F32)),
        grid=(N_TOK // tm,),
        in_specs=[
            pl.BlockSpec((tm, D_MODEL), lambda i: (i, 0)),
            _mod_spec(0, tm),
            _mod_spec(1, tm),
            pl.BlockSpec((D_MODEL, P_W), lambda i: (0, 0)),
            pl.BlockSpec((32, D_MODEL), lambda i: (0, 0)),
        ],
        out_specs=(pl.BlockSpec((tm, P_W), lambda i: (i, 0)),
                   pl.BlockSpec((2, 16, tm), lambda i: (0, 0, i))),
        compiler_params=_cparams(("parallel",)),
        name="inproj",
    )(x, mod_l, mod_l, w1, wt)


def _ctx_attn_kernel(sink_ref, na_ref, q_ref, k_ref, v_ref, ona_ref, osw_ref):
    scale = HEAD_DIM ** -0.5
    for h in range(NA_HEADS):
        q = na_ref[:, h * 64:(h + 1) * 64]
        k = na_ref[:, 256 + h * 64:256 + (h + 1) * 64]
        v = na_ref[:, 512 + h * 64:512 + (h + 1) * 64]
        s = _dot_nt(q, k) * scale
        m = jnp.max(s, axis=1, keepdims=True)
        p = jnp.exp(s - m)
        l = jnp.sum(p, axis=1, keepdims=True)
        ona_ref[:, h * 64:(h + 1) * 64] = _dot(p, v) / l
    for h in range(SW_HEADS):
        kh = h // (SW_HEADS // SW_KV_HEADS)
        q = q_ref[:, h * 64:(h + 1) * 64]
        k = k_ref[:, kh * 64:(kh + 1) * 64]
        v = v_ref[:, kh * 64:(kh + 1) * 64]
        snk = sink_ref[h]
        s = _dot_nt(q, k) * scale
        m = jnp.maximum(jnp.max(s, axis=1, keepdims=True), snk)
        p = jnp.exp(s - m)
        l = jnp.sum(p, axis=1, keepdims=True) + jnp.exp(snk - m)
        osw_ref[:, h * 64:(h + 1) * 64] = _dot(p, v) / l


def _ctx_attn(p, sink):
    t = SEQ
    return pl.pallas_call(
        _ctx_attn_kernel,
        out_shape=(jax.ShapeDtypeStruct((N_CTX, BRANCH_W), F32),
                   jax.ShapeDtypeStruct((N_CTX, BRANCH_W), F32)),
        grid=(BATCH,),
        in_specs=[
            pl.BlockSpec(memory_space=pltpu.SMEM),
            pl.BlockSpec((t, 768), lambda b: (b, P_NA // 768)),
            pl.BlockSpec((t, 256), lambda b: (b, P_SWQ // 256)),
            pl.BlockSpec((t, 128), lambda b: (b, P_SWK // 128)),
            pl.BlockSpec((t, 128), lambda b: (b, P_SWV // 128)),
        ],
        out_specs=(pl.BlockSpec((t, BRANCH_W), lambda b: (b, 0)),
                   pl.BlockSpec((t, BRANCH_W), lambda b: (b, 0))),
        compiler_params=_cparams(("parallel",)),
        name="ctx_attn",
    )(sink, p, p, p, p)


NA_ROWS = DEC_SEQ // GRID_W
NA_KEYS = NA_WR * GRID_W


def _lat_na_kernel(q_ref, k_ref, v_ref, ck_ref, cv_ref, bias_ref, o_ref):
    scale = HEAD_DIM ** -0.5
    r = pl.program_id(1)
    rs = jnp.clip(r - NA_WR // 2, 0, NA_ROWS - NA_WR)
    start = pl.multiple_of(rs * GRID_W, GRID_W)
    kw = k_ref[pl.ds(start, NA_KEYS), :]
    vw = v_ref[pl.ds(start, NA_KEYS), :]
    for h in range(NA_HEADS):
        sl = slice(h * 64, (h + 1) * 64)
        q = q_ref[:, sl]
        s_loc = _dot_nt(q, kw[:, sl]) * scale + bias_ref[h]
        s_ctx = _dot_nt(q, ck_ref[:, sl]) * scale
        m = jnp.maximum(jnp.max(s_loc, axis=1, keepdims=True), jnp.max(s_ctx, axis=1, keepdims=True))
        p_loc = jnp.exp(s_loc - m)
        p_ctx = jnp.exp(s_ctx - m)
        l = jnp.sum(p_loc, axis=1, keepdims=True) + jnp.sum(p_ctx, axis=1, keepdims=True)
        o_ref[:, sl] = (_dot(p_loc, vw[:, sl]) + _dot(p_ctx, cv_ref[:, sl])) / l


def _lat_na(p, ck, cv, bias):
    qb0 = N_CTX // GRID_W
    kb0 = N_CTX // DEC_SEQ
    return pl.pallas_call(
        _lat_na_kernel,
        out_shape=jax.ShapeDtypeStruct((N_LAT, BRANCH_W), F32),
        grid=(DEC_BATCH, NA_ROWS),
        in_specs=[
            pl.BlockSpec((GRID_W, 256), lambda b, r: (qb0 + b * NA_ROWS + r, 0)),
            pl.BlockSpec((DEC_SEQ, 256), lambda b, r: (kb0 + b, 1)),
            pl.BlockSpec((DEC_SEQ, 256), lambda b, r: (kb0 + b, 2)),
            pl.BlockSpec((None, PAST_LEN, 256), lambda b, r: (b, 0, 0)),
            pl.BlockSpec((None, PAST_LEN, 256), lambda b, r: (b, 0, 0)),
            pl.BlockSpec((NA_HEADS, None, GRID_W, NA_KEYS), lambda b, r: (0, r, 0, 0)),
        ],
        out_specs=pl.BlockSpec((GRID_W, BRANCH_W), lambda b, r: (b * NA_ROWS + r, 0)),
        compiler_params=_cparams(("parallel", "arbitrary")),
        name="lat_na",
    )(p, p, p, ck, cv, bias)


def _na_bias_table(rpb):
    r = np.arange(NA_ROWS)
    rs = np.clip(r - NA_WR // 2, 0, NA_ROWS - NA_WR)
    dr = rs[:, None] + np.arange(NA_WR)[None, :] - r[:, None] + NA_WR - 1
    qc = np.arange(GRID_W)[:, None]
    kc = np.arange(GRID_W)[None, :]
    cs = np.clip(qc - NA_WC // 2, 0, GRID_W - NA_WC)
    ok = (kc >= cs) & (kc < cs + NA_WC)
    dc = np.clip(kc - qc + NA_WC - 1, 0, 2 * NA_WC - 2)
    oh_r = (dr.reshape(-1)[:, None] == np.arange(2 * NA_WR - 1)[None, :]).astype(np.float32)
    oh_c = (np.arange(2 * NA_WC - 1)[:, None] == dc.reshape(-1)[None, :]).astype(np.float32)
    b = jnp.einsum('pd,hdc->hpc', jnp.asarray(oh_r), rpb.astype(F32), precision=HIGHEST)
    b = jnp.einsum('hpc,cq->hpq', b, jnp.asarray(oh_c), precision=HIGHEST)
    b = b.reshape(NA_HEADS, NA_ROWS, NA_WR, GRID_W, GRID_W).transpose(0, 1, 3, 2, 4)
    b = jnp.where(jnp.asarray(ok)[None, None, :, None, :], b, -jnp.inf)
    return b.reshape(NA_HEADS, NA_ROWS, GRID_W, NA_KEYS)


SW_NB = DEC_SEQ // SW_BLOCK
SW_KEYS = 3 * SW_BLOCK


def _rope(x, cos, sin):
    w = x.shape[1]
    lane = lax.broadcasted_iota(jnp.int32, x.shape, 1)
    first = (lane % 32) < 16
    swapped = jnp.where(first, pltpu.roll(x, w - 16, 1), pltpu.roll(x, 16, 1))
    return x * cos + swapped * sin


def _lat_swa_kernel(sink_ref, q_ref, k_ref, v_ref, ck_ref, cv_ref, cos_ref, sin_ref, o_ref):
    scale = HEAD_DIM ** -0.5
    n = pl.program_id(1)
    q0 = pl.multiple_of(n * SW_BLOCK, SW_BLOCK)
    start = pl.multiple_of(jnp.clip((n - 1) * SW_BLOCK, 0, DEC_SEQ - SW_KEYS), SW_BLOCK)
    q = _rope(q_ref[...], cos_ref[pl.ds(q0, SW_BLOCK), :], sin_ref[pl.ds(q0, SW_BLOCK), :])
    kw = _rope(k_ref[pl.ds(start, SW_KEYS), :], cos_ref[pl.ds(start, SW_KEYS), 0:128],
               sin_ref[pl.ds(start, SW_KEYS), 0:128])
    vw = v_ref[pl.ds(start, SW_KEYS), :]
    qpos = q0 + lax.broadcasted_iota(jnp.int32, (SW_BLOCK, SW_KEYS), 0)
    kpos = start + lax.broadcasted_iota(jnp.int32, (SW_BLOCK, SW_KEYS), 1)
    ok = jnp.abs(qpos - kpos) <= SW_WINDOW
    for h in range(SW_HEADS):
        kh = h // (SW_HEADS // SW_KV_HEADS)
        ksl = slice(kh * 64, (kh + 1) * 64)
        qh = q[:, h * 64:(h + 1) * 64]
        snk = sink_ref[h]
        s_loc = jnp.where(ok, _dot_nt(qh, kw[:, ksl]) * scale, -jnp.inf)
        s_ctx = _dot_nt(qh, ck_ref[:, ksl]) * scale
        m = jnp.maximum(jnp.maximum(jnp.max(s_loc, axis=1, keepdims=True),
                                    jnp.max(s_ctx, axis=1, keepdims=True)), snk)
        p_loc = jnp.exp(s_loc - m)
        p_ctx = jnp.exp(s_ctx - m)
        l = (jnp.sum(p_loc, axis=1, keepdims=True) + jnp.sum(p_ctx, axis=1, keepdims=True)
             + jnp.exp(snk - m))
        o_ref[:, h * 64:(h + 1) * 64] = (_dot(p_loc, vw[:, ksl]) + _dot(p_ctx, cv_ref[:, ksl])) / l


def _lat_swa(p, sink, ck, cv, cos, sin):
    qb0 = N_CTX // SW_BLOCK
    kb0 = N_CTX // DEC_SEQ
    return pl.pallas_call(
        _lat_swa_kernel,
        out_shape=jax.ShapeDtypeStruct((N_LAT, BRANCH_W), F32),
        grid=(DEC_BATCH, SW_NB),
        in_specs=[
            pl.BlockSpec(memory_space=pltpu.SMEM),
            pl.BlockSpec((SW_BLOCK, 256), lambda b, n: (qb0 + b * SW_NB + n, P_SWQ // 256)),
            pl.BlockSpec((DEC_SEQ, 128), lambda b, n: (kb0 + b, P_SWK // 128)),
            pl.BlockSpec((DEC_SEQ, 128), lambda b, n: (kb0 + b, P_SWV // 128)),
            pl.BlockSpec((None, PAST_LEN, 128), lambda b, n: (b, 0, 0)),
            pl.BlockSpec((None, PAST_LEN, 128), lambda b, n: (b, 0, 0)),
            pl.BlockSpec((DEC_SEQ, 256), lambda b, n: (0, 0)),
            pl.BlockSpec((DEC_SEQ, 256), lambda b, n: (0, 0)),
        ],
        out_specs=pl.BlockSpec((SW_BLOCK, BRANCH_W), lambda b, n: (b * SW_NB + n, 0)),
        compiler_params=_cparams(("parallel", "arbitrary")),
        name="lat_swa",
    )(sink, p, p, p, ck, cv, cos, sin)


def _rope_tables():
    t = np.arange(DEC_SEQ)
    half = 16
    freqs = ROPE_BASE ** (-np.arange(half, dtype=np.float64) / half)
    ang_r = (t // GRID_W)[:, None] * freqs[None, :]
    ang_c = (t % GRID_W)[:, None] * freqs[None, :]
    cos = np.concatenate([np.cos(ang_r), np.cos(ang_r), np.cos(ang_c), np.cos(ang_c)], axis=1)
    sin = np.concatenate([-np.sin(ang_r), np.sin(ang_r), -np.sin(ang_c), np.sin(ang_c)], axis=1)
    return (jnp.asarray(np.tile(cos, (1, 4)), F32), jnp.asarray(np.tile(sin, (1, 4)), F32))


def _dir_masks(d, n):
    row = lax.broadcasted_iota(jnp.int32, (n, n), 0)
    col = lax.broadcasted_iota(jnp.int32, (n, n), 1)
    u = (col - row) * (1 - 2 * d)
    return row, col, u


def _seq_specs(nblk, b0_blk):
    def blk(b, d, j):
        return b0_blk + b * nblk + jnp.where(d == 0, j, nblk - 1 - j)
    return blk


def _mlstm_kernel(qkv_ref, gt_ref, gc_ref, c0_ref, m0_ref, br_ref, bc_ref,
                  h_ref, cfin_ref, mfin_ref, c_s, m_s, *, nblk):
    n = SEQ_BLK
    d = pl.program_id(1)
    j = pl.program_id(2)

    @pl.when(j == 0)
    def _():
        c_s[...] = c0_ref[...]
        m_s[...] = m0_ref[...]

    _, col, u = _dir_masks(d, n)
    incl = u <= 0
    tri_c = jnp.where(incl, 1.0, 0.0).astype(F32)
    tri_r = jnp.where(u >= 0, 1.0, 0.0).astype(F32)
    end_lane = jnp.where(d == 0, n - 1, 0)

    g_r = gt_ref[0:8, :] + br_ref[...]
    f_r = _log_sigmoid(g_r)
    bcum_r = _dot_f32(f_r, tri_r)
    g_c = gc_ref[...] + bc_ref[...]
    f_c = _log_sigmoid(g_c)
    bcum_c = _dot_f32(tri_c, f_c)

    lane_r = lax.broadcasted_iota(jnp.int32, (1, n), 1)
    ones_col = jnp.where(lax.broadcasted_iota(jnp.int32, (n, 64), 1) == 0, 1.0, 0.0).astype(F32)

    hs = range(ML_HEADS)
    q = [qkv_ref[:, h * 64:(h + 1) * 64] for h in hs]
    k = [qkv_ref[:, 256 + h * 64:256 + (h + 1) * 64] * (64 ** -0.5) for h in hs]
    vp = [jnp.concatenate([qkv_ref[:, 512 + h * 64:512 + (h + 1) * 64], ones_col], axis=1)
          for h in hs]
    b_r = [bcum_r[4 + h:5 + h, :] for h in hs]
    i_c = [g_c[:, h:h + 1] for h in hs]
    b_c = [bcum_c[:, 4 + h:5 + h] for h in hs]
    m_prev = [m_s[h][:, 0:1] for h in hs]
    cp = [c_s[h] for h in hs]
    r_row = [g_r[h:h + 1, :] - b_r[h] for h in hs]

    qk = [_dot_nt(q[h], k[h]) for h in hs]
    qc = [_dot(q[h], cp[h]) for h in hs]
    inter = [b_c[h] + m_prev[h] for h in hs]
    dmat = [jnp.where(incl, b_c[h] + r_row[h], -jnp.inf) for h in hs]
    mt = [jnp.maximum(inter[h], jnp.max(dmat[h], axis=1, keepdims=True)) for h in hs]
    s = [qk[h] * jnp.exp(dmat[h] - mt[h]) for h in hs]
    numden = [_dot(s[h], vp[h]) + jnp.exp(inter[h] - mt[h]) * qc[h] for h in hs]
    for h in hs:
        num = numden[h][:, 0:64]
        den = numden[h][:, 64:65]
        h_ref[:, h * 64:(h + 1) * 64] = num / jnp.maximum(jnp.abs(den), jnp.exp(-mt[h]))

    b_last = [jnp.sum(jnp.where(lane_r == end_lane, b_r[h], 0.0), axis=1, keepdims=True)
              for h in hs]
    m_new = [jnp.maximum(b_last[h] + m_prev[h], jnp.max(b_last[h] + r_row[h], axis=1, keepdims=True))
             for h in hs]
    kw = [k[h] * jnp.exp(b_last[h] + (i_c[h] - b_c[h]) - m_new[h]) for h in hs]
    upd = [_dot_tn(kw[h], vp[h]) for h in hs]
    for h in hs:
        c_s[h] = jnp.exp(b_last[h] + m_prev[h] - m_new[h]) * cp[h] + upd[h]
        m_s[h] = jnp.broadcast_to(m_new[h], (1, 128))

    @pl.when(j == nblk - 1)
    def _():
        cfin_ref[...] = c_s[...]
        mfin_ref[...] = m_s[...]


def _mlstm(p, gt, c0p, m0p, br, bc, nb, t, tok0):
    nblk = t // SEQ_BLK
    blk = _seq_specs(nblk, tok0 // SEQ_BLK)
    blk_out = _seq_specs(nblk, 0)
    ntok = nb * t
    return pl.pallas_call(
        functools.partial(_mlstm_kernel, nblk=nblk),
        out_shape=(jax.ShapeDtypeStruct((2, ntok, 256), F32),
                   jax.ShapeDtypeStruct((nb, 2, ML_HEADS, 64, 128), F32),
                   jax.ShapeDtypeStruct((nb, 2, ML_HEADS, 1, 128), F32)),
        grid=(nb, 2, nblk),
        in_specs=[
            pl.BlockSpec((SEQ_BLK, 768), lambda b, d, j: (blk(b, d, j), P_ML // 768)),
            pl.BlockSpec((None, 16, SEQ_BLK), lambda b, d, j: (d, 0, blk(b, d, j))),
            pl.BlockSpec((SEQ_BLK, 128), lambda b, d, j: (blk(b, d, j), P_GATE // 128 + d)),
            pl.BlockSpec((None, None, ML_HEADS, 64, 128), lambda b, d, j: (b, d, 0, 0, 0)),
            pl.BlockSpec((None, None, ML_HEADS, 1, 128), lambda b, d, j: (b, d, 0, 0, 0)),
            pl.BlockSpec((None, 8, 1), lambda b, d, j: (d, 0, 0)),
            pl.BlockSpec((None, 1, 128), lambda b, d, j: (d, 0, 0)),
        ],
        out_specs=(pl.BlockSpec((None, SEQ_BLK, 256), lambda b, d, j: (d, blk_out(b, d, j), 0)),
                   pl.BlockSpec((None, None, ML_HEADS, 64, 128), lambda b, d, j: (b, d, 0, 0, 0)),
                   pl.BlockSpec((None, None, ML_HEADS, 1, 128), lambda b, d, j: (b, d, 0, 0, 0))),
        scratch_shapes=[pltpu.VMEM((ML_HEADS, 64, 128), F32), pltpu.VMEM((ML_HEADS, 1, 128), F32)],
        compiler_params=_cparams(("parallel", "arbitrary", "arbitrary")),
        name="mlstm",
    )(p, gt, p, c0p, m0p, br, bc)


def _gdn_pre_kernel(x_ref, w_ref, o_ref, pad_s, *, t):
    pad_s[0:8, :] = jnp.zeros((8, 768), F32)
    pad_s[t + 8:t + 16, :] = jnp.zeros((8, 768), F32)
    pad_s[8:t + 8, :] = x_ref[...]
    y = jnp.zeros((t, 768), F32)
    for jj in range(DN_CONV):
        off = 8 + jj - DN_CONV // 2
        y = y + pad_s[off:off + t, :] * w_ref[jj:jj + 1, :]
    a = y * _sigmoid(y)
    for h in range(DN_HEADS):
        for part, mul in ((0, 64 ** -0.5), (256, 1.0)):
            sl = slice(part + h * 64, part + (h + 1) * 64)
            z = a[:, sl]
            o_ref[:, sl] = z * lax.rsqrt(jnp.sum(z * z, axis=1, keepdims=True) + NORM_EPS) * mul
    o_ref[:, 512:768] = a[:, 512:768]


def _gdn_pre(p, conv_w, nb, t, tok0):
    b0 = tok0 // t
    return pl.pallas_call(
        functools.partial(_gdn_pre_kernel, t=t),
        out_shape=jax.ShapeDtypeStruct((nb * t, 768), F32),
        grid=(nb,),
        in_specs=[
            pl.BlockSpec((t, 768), lambda b: (b0 + b, P_DN // 768)),
            pl.BlockSpec((8, 768), lambda b: (0, 0)),
        ],
        out_specs=pl.BlockSpec((t, 768), lambda b: (b, 0)),
        scratch_shapes=[pltpu.VMEM((t + 16, 768), F32)],
        compiler_params=_cparams(("parallel",)),
        name="gdn_pre",
    )(p, conv_w)


def _gdn_kernel(qkv_ref, gt_ref, gc_ref, s0_ref, pr_ref, pc_ref, o_ref, sfin_ref, s_s, *, nblk):
    n = SEQ_BLK
    d = pl.program_id(1)
    j = pl.program_id(2)

    @pl.when(j == 0)
    def _():
        s_s[...] = s0_ref[...]

    row, col, u = _dir_masks(d, n)
    incl = u <= 0
    strict = u < 0
    tri_c = jnp.where(incl, 1.0, 0.0).astype(F32)
    tri_r = jnp.where(u >= 0, 1.0, 0.0).astype(F32)
    end_lane = jnp.where(d == 0, n - 1, 0)
    eye = jnp.where(u == 0, 1.0, 0.0).astype(F32)

    a_r = gt_ref[8:16, :]
    g_r = -jnp.exp(pr_ref[0:8, :]) * _softplus(a_r + pr_ref[8:16, :])
    gcum_r = _dot_f32(g_r, tri_r)
    a_c = gc_ref[...]
    g_c = -jnp.exp(pc_ref[1:2, :]) * _softplus(a_c + pc_ref[0:1, :])
    gcum_c = _dot_f32(tri_c, g_c)
    beta_c = _sigmoid(a_c)
    lane_r = lax.broadcasted_iota(jnp.int32, (1, n), 1)

    hs = range(DN_HEADS)
    qn = [qkv_ref[:, h * 64:(h + 1) * 64] for h in hs]
    kn = [qkv_ref[:, 256 + h * 64:256 + (h + 1) * 64] for h in hs]
    vv = [qkv_ref[:, 512 + h * 64:512 + (h + 1) * 64] for h in hs]
    gc_row = [gcum_r[h:h + 1, :] for h in hs]
    gc_col = [gcum_c[:, 8 + h:9 + h] for h in hs]
    beta = [beta_c[:, 12 + h:13 + h] for h in hs]
    s_prev = [s_s[h] for h in hs]

    decay = [jnp.exp(jnp.where(incl, gc_col[h] - gc_row[h], -jnp.inf)) for h in hs]
    kb = [kn[h] * beta[h] for h in hs]
    gram = [_dot_nt(kb[h], kn[h]) for h in hs]
    a_mat = [jnp.where(strict, gram[h] * decay[h], 0.0) for h in hs]
    e_col = [jnp.exp(gc_col[h]) for h in hs]
    rhs = [jnp.concatenate([vv[h] * beta[h], kb[h] * e_col[h]], axis=1) for h in hs]

    same2 = (row >> 1) == (col >> 1)
    tinv = [eye - jnp.where(same2, a_mat[h], 0.0) for h in hs]
    for lb in range(1, 8):
        inner = (row >> lb) == (col >> lb)
        outer = (row >> (lb + 1)) == (col >> (lb + 1))
        off = jnp.logical_and(outer, jnp.logical_not(inner))
        m1 = [_dot(jnp.where(off, a_mat[h], 0.0), tinv[h]) for h in hs]
        tinv = [tinv[h] - _dot(tinv[h], m1[h]) for h in hs]
    sol = [_dot(tinv[h], rhs[h]) for h in hs]
    qk = [_dot_nt(qn[h], kn[h]) * decay[h] for h in hs]
    g_last = [jnp.sum(jnp.where(lane_r == end_lane, gc_row[h], 0.0), axis=1, keepdims=True)
              for h in hs]
    v_new = [sol[h][:, 0:64] - _dot(sol[h][:, 64:128], s_prev[h]) for h in hs]
    for h in hs:
        o_ref[:, h * 64:(h + 1) * 64] = _dot(qn[h] * e_col[h], s_prev[h]) + _dot(qk[h], v_new[h])
    for h in hs:
        k_dec = kn[h] * jnp.exp(g_last[h] - gc_col[h])
        s_s[h] = jnp.exp(g_last[h]) * s_prev[h] + _dot_tn(k_dec, v_new[h])

    @pl.when(j == nblk - 1)
    def _():
        sfin_ref[...] = s_s[...]


def _gdn(qkv, p, gt, s0, pr, pc, nb, t, tok0):
    nblk = t // SEQ_BLK
    blk = _seq_specs(nblk, tok0 // SEQ_BLK)
    blk0 = _seq_specs(nblk, 0)
    ntok = nb * t
    return pl.pallas_call(
        functools.partial(_gdn_kernel, nblk=nblk),
        out_shape=(jax.ShapeDtypeStruct((2, ntok, 256), F32),
                   jax.ShapeDtypeStruct((nb, 2, DN_HEADS, 64, 64), F32)),
        grid=(nb, 2, nblk),
        in_specs=[
            pl.BlockSpec((SEQ_BLK, 768), lambda b, d, j: (blk0(b, d, j), 0)),
            pl.BlockSpec((None, 16, SEQ_BLK), lambda b, d, j: (d, 0, blk(b, d, j))),
            pl.BlockSpec((SEQ_BLK, 128), lambda b, d, j: (blk(b, d, j), P_GATE // 128 + d)),
            pl.BlockSpec((None, None, DN_HEADS, 64, 64), lambda b, d, j: (b, d, 0, 0, 0)),
            pl.BlockSpec((None, 16, 1), lambda b, d, j: (d, 0, 0)),
            pl.BlockSpec((None, 2, 128), lambda b, d, j: (d, 0, 0)),
        ],
        out_specs=(pl.BlockSpec((None, SEQ_BLK, 256), lambda b, d, j: (d, blk0(b, d, j), 0)),
                   pl.BlockSpec((None, None, DN_HEADS, 64, 64), lambda b, d, j: (b, d, 0, 0, 0))),
        scratch_shapes=[pltpu.VMEM((DN_HEADS, 64, 64), F32)],
        compiler_params=_cparams(("parallel", "arbitrary", "arbitrary")),
        name="gdn",
    )(qkv, gt, p, s0, pr, pc)


MG_TM = 512


def _merge_kernel(x_ref, sh_ref, sc_ref, gt_ref, ona_ref, osw_ref, hf_ref, hb_ref, mlo_ref,
                  df_ref, db_ref, dng_ref, mlg_ref, dngain_ref, wmg_ref, wb_ref, wo_ref,
                  lng_ref, lnb_ref, o_ref):
    x = x_ref[...]
    xm = (x * (1.0 + sc_ref[...]) + sh_ref[...]).astype(BF16)
    hh = hf_ref[...] + hb_ref[...]
    dd = df_ref[...] + db_ref[...]
    ml_parts = []
    dn_parts = []
    for h in range(ML_HEADS):
        sl = slice(h * 64, (h + 1) * 64)
        z = hh[:, sl]
        mu = jnp.mean(z, axis=1, keepdims=True)
        zc = z - mu
        var = jnp.mean(zc * zc, axis=1, keepdims=True)
        ml_parts.append(zc * lax.rsqrt(var + LN_EPS))
        z = dd[:, sl]
        dn_parts.append(z * lax.rsqrt(jnp.mean(z * z, axis=1, keepdims=True) + NORM_EPS))
    o_ml = jnp.concatenate(ml_parts, axis=1) * mlg_ref[...] * _sigmoid(mlo_ref[...])
    g_pre = dng_ref[...]
    o_dn = jnp.concatenate(dn_parts, axis=1) * dngain_ref[...] * (g_pre * _sigmoid(g_pre))
    acc = jnp.zeros((x.shape[0], D_MODEL), F32)
    for nbr, o_n in enumerate((ona_ref[...], o_ml, o_dn, osw_ref[...])):
        y = _dot(o_n, wb_ref[nbr])
        gate = _sigmoid(jnp.dot(xm, wmg_ref[:, nbr * D_MODEL:(nbr + 1) * D_MODEL],
                                preferred_element_type=F32))
        acc = acc + gate * y
    mix = _dot(acc, wo_ref[...])
    o_ref[...] = _layer_norm(ALPHA * x + gt_ref[...] * mix, lng_ref[...], lnb_ref[...])


def _merge(x, mod_l, p, o_na, o_sw, hml, odn, ml_gain, dn_gain, wmg, wb, wo, ln_g, ln_b):
    tm = MG_TM
    row = lambda i: (i, 0)
    const2 = lambda i: (0, 0)
    return pl.pallas_call(
        _merge_kernel,
        out_shape=jax.ShapeDtypeStruct((N_TOK, D_MODEL), F32),
        grid=(N_TOK // tm,),
        in_specs=[
            pl.BlockSpec((tm, D_MODEL), row),
            _mod_spec(0, tm), _mod_spec(1, tm), _mod_spec(2, tm),
            pl.BlockSpec((tm, 256), row),
            pl.BlockSpec((tm, 256), row),
            pl.BlockSpec((None, tm, 256), lambda i: (0, i, 0)),
            pl.BlockSpec((None, tm, 256), lambda i: (1, i, 0)),
            pl.BlockSpec((tm, 256), lambda i: (i, P_MLO // 256)),
            pl.BlockSpec((None, tm, 256), lambda i: (0, i, 0)),
            pl.BlockSpec((None, tm, 256), lambda i: (1, i, 0)),
            pl.BlockSpec((tm, 256), lambda i: (i, P_DNG // 256)),
            pl.BlockSpec((1, 256), const2),
            pl.BlockSpec((1, 256), const2),
            pl.BlockSpec((D_MODEL, N_BRANCH * D_MODEL), const2),
            pl.BlockSpec((N_BRANCH, BRANCH_W, D_MODEL), lambda i: (0, 0, 0)),
            pl.BlockSpec((D_MODEL, D_MODEL), const2),
            pl.BlockSpec((1, D_MODEL), const2),
            pl.BlockSpec((1, D_MODEL), const2),
        ],
        out_specs=pl.BlockSpec((tm, D_MODEL), row),
        compiler_params=_cparams(("parallel",)),
        name="merge",
    )(x, mod_l, mod_l, mod_l, o_na, o_sw, hml, hml, p, odn, odn, p, ml_gain, dn_gain,
      wmg, wb, wo, ln_g, ln_b)


FF_TM = 512
FF_TH = 1408
FF_NK = FFN_DIM // FF_TH


def _ffn_kernel(x_ref, sh_ref, sc_ref, gt_ref, wg_ref, wu_ref, wo_ref, lng_ref, lnb_ref, o_ref,
                xm_s, acc_s):
    k = pl.program_id(1)

    @pl.when(k == 0)
    def _():
        xm_s[...] = (x_ref[...] * (1.0 + sc_ref[...]) + sh_ref[...]).astype(BF16)
        acc_s[...] = jnp.zeros_like(acc_s)

    xm = xm_s[...]
    hg = jnp.dot(xm, wg_ref[...], preferred_element_type=F32)
    hu = jnp.dot(xm, wu_ref[...], preferred_element_type=F32)
    a = hg * _sigmoid(hg) * hu
    acc_s[...] += _dot(a, wo_ref[...])

    @pl.when(k == FF_NK - 1)
    def _():
        z = ALPHA * x_ref[...] + gt_ref[...] * acc_s[...]
        o_ref[...] = _layer_norm(z, lng_ref[...], lnb_ref[...])


def _ffn(x, mod_l, w_in, w_out, ln_g, ln_b):
    tm = FF_TM
    return pl.pallas_call(
        _ffn_kernel,
        out_shape=jax.ShapeDtypeStruct((N_TOK, D_MODEL), F32),
        grid=(N_TOK // tm, FF_NK),
        in_specs=[
            pl.BlockSpec((tm, D_MODEL), lambda i, k: (i, 0)),
            _mod_spec(3, tm), _mod_spec(4, tm), _mod_spec(5, tm),
            pl.BlockSpec((D_MODEL, FF_TH), lambda i, k: (0, k)),
            pl.BlockSpec((D_MODEL, FF_TH), lambda i, k: (0, FF_NK + k)),
            pl.BlockSpec((FF_TH, D_MODEL), lambda i, k: (k, 0)),
            pl.BlockSpec((1, D_MODEL), lambda i, k: (0, 0)),
            pl.BlockSpec((1, D_MODEL), lambda i, k: (0, 0)),
        ],
        out_specs=pl.BlockSpec((tm, D_MODEL), lambda i, k: (i, 0)),
        scratch_shapes=[pltpu.VMEM((tm, D_MODEL), BF16), pltpu.VMEM((tm, D_MODEL), F32)],
        compiler_params=_cparams(("parallel", "arbitrary")),
        name="ffn",
    )(x, mod_l, mod_l, mod_l, w_in, w_in, w_out, ln_g, ln_b)


def _router_kernel(x_ref, sh_ref, sc_ref, wr_ref, br_ref, g_ref):
    xm = x_ref[...] * (1.0 + sc_ref[...]) + sh_ref[...]
    logits = _dot_f32(xm, wr_ref[...]) + br_ref[...]
    lane = lax.broadcasted_iota(jnp.int32, logits.shape, 1)
    neg = jnp.where(lane < N_EXPERTS, logits, -jnp.inf)
    v1 = jnp.max(neg, axis=1, keepdims=True)
    i1 = jnp.min(jnp.where(neg == v1, lane, 128), axis=1, keepdims=True)
    rest = jnp.where(lane == i1, -jnp.inf, neg)
    v2 = jnp.max(rest, axis=1, keepdims=True)
    i2 = jnp.min(jnp.where(rest == v2, lane, 128), axis=1, keepdims=True)
    e2 = jnp.exp(v2 - v1)
    p1 = 1.0 / (1.0 + e2)
    p2 = e2 / (1.0 + e2)
    g_ref[...] = jnp.where(lane == i1, p1, 0.0) + jnp.where(lane == i2, p2, 0.0)


def _router(x, mod_l, wr, br):
    tm = 512
    return pl.pallas_call(
        _router_kernel,
        out_shape=jax.ShapeDtypeStruct((N_TOK, 128), F32),
        grid=(N_TOK // tm,),
        in_specs=[
            pl.BlockSpec((tm, D_MODEL), lambda i: (i, 0)),
            _mod_spec(3, tm), _mod_spec(4, tm),
            pl.BlockSpec((D_MODEL, 128), lambda i: (0, 0)),
            pl.BlockSpec((1, 128), lambda i: (0, 0)),
        ],
        out_specs=pl.BlockSpec((tm, 128), lambda i: (i, 0)),
        compiler_params=_cparams(("parallel",)),
        name="router",
    )(x, mod_l, mod_l, wr, br)


def _moe_kernel(x_ref, sh_ref, sc_ref, gt_ref, gates_ref, wg_ref, wu_ref, wo_ref, lng_ref, lnb_ref,
                o_ref, xm_s, acc_s):
    e = pl.program_id(1)
    k = pl.program_id(2)

    @pl.when(jnp.logical_and(e == 0, k == 0))
    def _():
        xm_s[...] = (x_ref[...] * (1.0 + sc_ref[...]) + sh_ref[...]).astype(BF16)
        acc_s[...] = jnp.zeros_like(acc_s)

    xm = xm_s[...]
    hg = jnp.dot(xm, wg_ref[...], preferred_element_type=F32)
    hu = jnp.dot(xm, wu_ref[...], preferred_element_type=F32)
    a = hg * _sigmoid(hg) * hu
    lane = lax.broadcasted_iota(jnp.int32, gates_ref.shape, 1)
    gate = jnp.sum(jnp.where(lane == e, gates_ref[...], 0.0), axis=1, keepdims=True)
    acc_s[...] += gate * _dot(a, wo_ref[...])

    @pl.when(jnp.logical_and(e == N_EXPERTS - 1, k == FF_NK - 1))
    def _():
        z = ALPHA * x_ref[...] + gt_ref[...] * acc_s[...]
        o_ref[...] = _layer_norm(z, lng_ref[...], lnb_ref[...])


def _moe(x, mod_l, gates, w_in, w_out, ln_g, ln_b):
    tm = FF_TM
    return pl.pallas_call(
        _moe_kernel,
        out_shape=jax.ShapeDtypeStruct((N_TOK, D_MODEL), F32),
        grid=(N_TOK // tm, N_EXPERTS, FF_NK),
        in_specs=[
            pl.BlockSpec((tm, D_MODEL), lambda i, e, k: (i, 0)),
            _mod_spec(3, tm), _mod_spec(4, tm), _mod_spec(5, tm),
            pl.BlockSpec((tm, 128), lambda i, e, k: (i, 0)),
            pl.BlockSpec((None, D_MODEL, FF_TH), lambda i, e, k: (e, 0, k)),
            pl.BlockSpec((None, D_MODEL, FF_TH), lambda i, e, k: (e, 0, FF_NK + k)),
            pl.BlockSpec((None, FF_TH, D_MODEL), lambda i, e, k: (e, k, 0)),
            pl.BlockSpec((1, D_MODEL), lambda i, e, k: (0, 0)),
            pl.BlockSpec((1, D_MODEL), lambda i, e, k: (0, 0)),
        ],
        out_specs=pl.BlockSpec((tm, D_MODEL), lambda i, e, k: (i, 0)),
        scratch_shapes=[pltpu.VMEM((tm, D_MODEL), BF16), pltpu.VMEM((tm, D_MODEL), F32)],
        compiler_params=_cparams(("parallel", "arbitrary", "arbitrary")),
        name="moe",
    )(x, mod_l, mod_l, mod_l, gates, w_in, w_in, w_out, ln_g, ln_b)


def _split_w_in(w):
    na = w[:, 0:768]
    ml = w[:, 768:1536]
    mlg = w[:, 1536:1552]
    mlo = w[:, 1552:1808]
    dn = w[:, 1808:2576]
    dna = w[:, 2576:2584]
    dnb = w[:, 2584:2592]
    dng = w[:, 2592:2848]
    sw = w[:, 2848:3360]
    mg = w[:, 3360:7456]
    zpad = jnp.zeros((D_MODEL, 112), w.dtype)
    gates = []
    for d in range(2):
        gates.append(jnp.concatenate([mlg[:, 4 * d:4 * d + 4], mlg[:, 8 + 4 * d:12 + 4 * d],
                                      dna[:, 4 * d:4 * d + 4], dnb[:, 4 * d:4 * d + 4]], axis=1))
    w1 = jnp.concatenate([na, ml, dn, mlo, dng, sw, gates[0], zpad, gates[1], zpad], axis=1)
    wt = jnp.concatenate(gates, axis=1).T
    return w1.astype(BF16), wt.astype(BF16), mg.astype(BF16)


def kernel(x_prompt, x_sample, c, cache_na_k, cache_na_v, cache_sw_k, cache_sw_v, state_ml_c, state_ml_n, state_ml_m, state_dn_s, c_ctx, ada_w, ada_b, w_in, na_rpb, ml_gate_b, ml_norm_g, dn_conv_w, dn_a_log, dn_dt_bias, dn_norm_g, sw_sink, w_branch, w_out, ln_g, ln_b, ffn_w_in, ffn_w_out, moe_router, moe_router_b, moe_w_in, moe_w_out):
    x = jnp.concatenate([x_prompt.reshape(N_CTX, D_MODEL), x_sample.reshape(N_LAT, D_MODEL)], axis=0)
    cvec = jnp.concatenate([c_ctx[None, :], c, jnp.zeros((3, D_MODEL), F32)], axis=0)
    mod = _ada(cvec, ada_w, ada_b).reshape(DEPTH, 8, 6, 1, D_MODEL)
    cos, sin = _rope_tables()

    zc_ctx = jnp.zeros((BATCH, 2, ML_HEADS, 64, 128), F32)
    zm_ctx = jnp.zeros((BATCH, 2, ML_HEADS, 1, 128), F32)
    zs_ctx = jnp.zeros((BATCH, 2, DN_HEADS, 64, 64), F32)
    pad63 = jnp.zeros((DEC_BATCH, 2, ML_HEADS, 64, 63), F32)

    new = [[] for _ in range(8)]
    for l in range(DEPTH):
        w1, wt, wmg = _split_w_in(w_in[l])
        mod_l = mod[l]
        p, gt = _inproj(x, mod_l, w1, wt)

        o_na_c, o_sw_c = _ctx_attn(p, sw_sink[l])
        ck = cache_na_k[:, l].reshape(DEC_BATCH, PAST_LEN, 256)
        cv = cache_na_v[:, l].reshape(DEC_BATCH, PAST_LEN, 256)
        o_na_l = _lat_na(p, ck, cv, _na_bias_table(na_rpb[l]))
        sk = cache_sw_k[:, l].reshape(DEC_BATCH, PAST_LEN, 128)
        sv = cache_sw_v[:, l].reshape(DEC_BATCH, PAST_LEN, 128)
        o_sw_l = _lat_swa(p, sw_sink[l], sk, sv, cos, sin)

        gb = ml_gate_b[l]
        br = jnp.stack([jnp.concatenate([gb[0], gb[2]]), jnp.concatenate([gb[1], gb[3]])])[:, :, None]
        bc = jnp.pad(br[:, :, 0], ((0, 0), (0, 120)))[:, None, :]
        h_c, cfin, mfin = _mlstm(p, gt, zc_ctx, zm_ctx, br, bc, BATCH, SEQ, 0)
        c0p = jnp.concatenate([state_ml_c[:, l], state_ml_n[:, l][..., None], pad63], axis=-1)
        m0p = jnp.broadcast_to(state_ml_m[:, l][..., None, None], (DEC_BATCH, 2, ML_HEADS, 1, 128))
        h_l, _, _ = _mlstm(p, gt, c0p, m0p, br, bc, DEC_BATCH, DEC_SEQ, N_CTX)

        conv_w = jnp.pad(dn_conv_w[l], ((0, 3), (0, 0)))
        z4 = jnp.zeros((2, 4), F32)
        pr = jnp.concatenate([dn_a_log[l], z4, dn_dt_bias[l], z4], axis=1)[:, :, None]
        pc = jnp.stack([jnp.pad(dn_dt_bias[l], ((0, 0), (8, 116))),
                        jnp.pad(dn_a_log[l], ((0, 0), (8, 116)))], axis=1)
        qkv_c = _gdn_pre(p, conv_w, BATCH, SEQ, 0)
        o_c, sfin = _gdn(qkv_c, p, gt, zs_ctx, pr, pc, BATCH, SEQ, 0)
        qkv_l = _gdn_pre(p, conv_w, DEC_BATCH, DEC_SEQ, N_CTX)
        o_l, _ = _gdn(qkv_l, p, gt, state_dn_s[:, l], pr, pc, DEC_BATCH, DEC_SEQ, N_CTX)

        o_na = jnp.concatenate([o_na_c, o_na_l], axis=0)
        o_sw = jnp.concatenate([o_sw_c, o_sw_l], axis=0)
        hml = jnp.concatenate([h_c, h_l], axis=1)
        odn = jnp.concatenate([o_c, o_l], axis=1)
        x = _merge(x, mod_l, p, o_na, o_sw, hml, odn, ml_norm_g[l][None, :],
                   jnp.tile(dn_norm_g[l], DN_HEADS)[None, :], wmg, w_branch[l].astype(BF16),
                   w_out[l].astype(BF16), ln_g[l, 0][None, :], ln_b[l, 0][None, :])

        jx = l // 2
        if l % 2 == 0:
            x = _ffn(x, mod_l, ffn_w_in[jx].astype(BF16), ffn_w_out[jx].astype(BF16),
                     ln_g[l, 1][None, :], ln_b[l, 1][None, :])
        else:
            wr = jnp.pad(moe_router[jx], ((0, 0), (0, 120)))
            brt = jnp.pad(moe_router_b[jx], (0, 120))[None, :]
            gates = _router(x, mod_l, wr, brt)
            x = _moe(x, mod_l, gates, moe_w_in[jx].astype(BF16), moe_w_out[jx].astype(BF16),
                     ln_g[l, 1][None, :], ln_b[l, 1][None, :])

        pc_ = p[0:N_CTX]
        new[0].append(pc_[:, 256:512].reshape(BATCH, SEQ, NA_HEADS, HEAD_DIM))
        new[1].append(pc_[:, 512:768].reshape(BATCH, SEQ, NA_HEADS, HEAD_DIM))
        new[2].append(pc_[:, P_SWK:P_SWK + 128].reshape(BATCH, SEQ, SW_KV_HEADS, HEAD_DIM))
        new[3].append(pc_[:, P_SWV:P_SWV + 128].reshape(BATCH, SEQ, SW_KV_HEADS, HEAD_DIM))
        new[4].append(cfin[..., 0:64])
        new[5].append(cfin[..., 64])
        new[6].append(mfin[..., 0, 0])
        new[7].append(sfin)

    y_prompt = x[0:N_CTX].reshape(BATCH, SEQ, D_MODEL)
    y_sample = x[N_CTX:].reshape(DEC_BATCH, DEC_SEQ, D_MODEL)
    outs = [jnp.stack(s, axis=1) for s in new]
    return (y_prompt, y_sample) + tuple(outs)
```

```python
import functools
import math

import numpy as np
import jax
import jax.numpy as jnp
from jax import lax
from jax.experimental import pallas as pl
from jax.experimental.pallas import tpu as pltpu

F32 = jnp.float32
BF16 = jnp.bfloat16
HIGHEST = lax.Precision.HIGHEST

D_MODEL = 1024
BATCH = 16
SEQ = 256
DEPTH = 4
DEC_BATCH = 4
DEC_SEQ = 1024
PAST_LEN = 512
GRID_W = 64
HEAD_DIM = 64
NA_HEADS = 4
NA_WR = 8
NA_WC = 16
ML_HEADS = 4
DN_HEADS = 4
DN_CONV = 5
SW_HEADS = 4
SW_KV_HEADS = 2
SW_WINDOW = 128
SW_BLOCK = 128
ROPE_BASE = 10000.0
N_BRANCH = 4
FFN_DIM = 2816
N_EXPERTS = 8
ALPHA = (2 * DEPTH) ** 0.25
LN_EPS = 1e-5
NORM_EPS = 1e-6

N_CTX = BATCH * SEQ
N_LAT = DEC_BATCH * DEC_SEQ
N_TOK = N_CTX + N_LAT
BRANCH_W = NA_HEADS * HEAD_DIM

P_NA = 0
P_ML = 768
P_DN = 1536
P_MLO = 2304
P_DNG = 2560
P_SWQ = 2816
P_SWK = 3072
P_SWV = 3200
P_GATE = 3328
P_W = 3584

SEQ_BLK = 256
V7X_VMEM_LIMIT = 56 * 1024 * 1024


def _cparams(sem, vmem=V7X_VMEM_LIMIT):
    return pltpu.CompilerParams(dimension_semantics=sem, vmem_limit_bytes=vmem)


def _dot(a, b):
    return jnp.dot(a.astype(BF16), b.astype(BF16), preferred_element_type=F32)


def _dot_nt(a, b):
    return lax.dot_general(a.astype(BF16), b.astype(BF16), (((1,), (1,)), ((), ())),
                           preferred_element_type=F32)


def _dot_tn(a, b):
    return lax.dot_general(a.astype(BF16), b.astype(BF16), (((0,), (0,)), ((), ())),
                           preferred_element_type=F32)


def _dot_f32(a, b):
    return jnp.dot(a, b, precision=HIGHEST, preferred_element_type=F32)


def _sigmoid(x):
    return 1.0 / (1.0 + jnp.exp(-x))


def _softplus(x):
    return jnp.maximum(x, 0.0) + jnp.log(1.0 + jnp.exp(-jnp.abs(x)))


def _log_sigmoid(x):
    return -_softplus(-x)


def _mod_row(tok_start):
    return jnp.where(tok_start < N_CTX, 0, 1 + (tok_start - N_CTX) // DEC_SEQ)


def _layer_norm(z, g, b):
    mu = jnp.mean(z, axis=-1, keepdims=True)
    zc = z - mu
    var = jnp.mean(zc * zc, axis=-1, keepdims=True)
    return zc * lax.rsqrt(var + LN_EPS) * g + b


def _ada_kernel(c_ref, w_ref, b_ref, o_ref):
    c = c_ref[...]
    s = c * _sigmoid(c)
    o_ref[...] = _dot(s, w_ref[...]) + b_ref[...]


def _ada(cvec, ada_w, ada_b):
    tn = 1536
    n = 6 * D_MODEL
    return pl.pallas_call(
        _ada_kernel,
        out_shape=jax.ShapeDtypeStruct((DEPTH, 8, n), F32),
        grid=(DEPTH, n // tn),
        in_specs=[
            pl.BlockSpec((8, D_MODEL), lambda l, j: (0, 0)),
            pl.BlockSpec((None, D_MODEL, tn), lambda l, j: (l, 0, j)),
            pl.BlockSpec((None, 1, tn), lambda l, j: (l, 0, j)),
        ],
        out_specs=pl.BlockSpec((None, 8, tn), lambda l, j: (l, 0, j)),
        compiler_params=_cparams(("parallel", "parallel")),
        name="ada",
    )(cvec, ada_w, ada_b.reshape(DEPTH, 1, n))


def _mod_spec(chunk, tm):
    return pl.BlockSpec((None, None, 1, D_MODEL), lambda i, *_: (_mod_row(i * tm), chunk, 0, 0))


IN_TM = 512
IN_NCH = 4


def _inproj_kernel(x_ref, sh_ref, sc_ref, w_ref, wt_ref, p_ref, gt_ref):
    xm = (x_ref[...] * (1.0 + sc_ref[...]) + sh_ref[...]).astype(BF16)
    cw = P_W // IN_NCH
    for c in range(IN_NCH):
        p_ref[:, c * cw:(c + 1) * cw] = jnp.dot(xm, w_ref[:, c * cw:(c + 1) * cw],
                                                preferred_element_type=F32)
    gt = lax.dot_general(wt_ref[...], xm, (((1,), (1,)), ((), ())), preferred_element_type=F32)
    gt_ref[0] = gt[0:16]
    gt_ref[1] = gt[16:32]


def _inproj(x, mod_l, w1, wt):
    tm = IN_TM
    return pl.pallas_call(
        _inproj_kernel,
        out_shape=(jax.ShapeDtypeStruct((N_TOK, P_W), F32),
                   jax.ShapeDtypeStruct((2, 16, N_TOK), F32)),
        grid=(N_TOK // tm,),
        in_specs=[
            pl.BlockSpec((tm, D_MODEL), lambda i: (i, 0)),
            _mod_spec(0, tm),
            _mod_spec(1, tm),
            pl.BlockSpec((D_MODEL, P_W), lambda i: (0, 0)),
            pl.BlockSpec((32, D_MODEL), lambda i: (0, 0)),
        ],
        out_specs=(pl.BlockSpec((tm, P_W), lambda i: (i, 0)),
                   pl.BlockSpec((2, 16, tm), lambda i: (0, 0, i))),
        compiler_params=_cparams(("parallel",)),
        name="inproj",
    )(x, mod_l, mod_l, w1, wt)


def _ctx_attn_kernel(sink_ref, na_ref, q_ref, k_ref, v_ref, ona_ref, osw_ref):
    scale = HEAD_DIM ** -0.5
    for h in range(NA_HEADS):
        q = na_ref[:, h * 64:(h + 1) * 64]
        k = na_ref[:, 256 + h * 64:256 + (h + 1) * 64]
        v = na_ref[:, 512 + h * 64:512 + (h + 1) * 64]
        s = _dot_nt(q, k) * scale
        m = jnp.max(s, axis=1, keepdims=True)
        p = jnp.exp(s - m)
        l = jnp.sum(p, axis=1, keepdims=True)
        ona_ref[:, h * 64:(h + 1) * 64] = _dot(p, v) / l
    for h in range(SW_HEADS):
        kh = h // (SW_HEADS // SW_KV_HEADS)
        q = q_ref[:, h * 64:(h + 1) * 64]
        k = k_ref[:, kh * 64:(kh + 1) * 64]
        v = v_ref[:, kh * 64:(kh + 1) * 64]
        snk = sink_ref[h]
        s = _dot_nt(q, k) * scale
        m = jnp.maximum(jnp.max(s, axis=1, keepdims=True), snk)
        p = jnp.exp(s - m)
        l = jnp.sum(p, axis=1, keepdims=True) + jnp.exp(snk - m)
        osw_ref[:, h * 64:(h + 1) * 64] = _dot(p, v) / l


def _ctx_attn(p, sink):
    t = SEQ
    return pl.pallas_call(
        _ctx_attn_kernel,
        out_shape=(jax.ShapeDtypeStruct((N_CTX, BRANCH_W), F32),
                   jax.ShapeDtypeStruct((N_CTX, BRANCH_W), F32)),
        grid=(BATCH,),
        in_specs=[
            pl.BlockSpec(memory_space=pltpu.SMEM),
            pl.BlockSpec((t, 768), lambda b: (b, P_NA // 768)),
            pl.BlockSpec((t, 256), lambda b: (b, P_SWQ // 256)),
            pl.BlockSpec((t, 128), lambda b: (b, P_SWK // 128)),
            pl.BlockSpec((t, 128), lambda b: (b, P_SWV // 128)),
        ],
        out_specs=(pl.BlockSpec((t, BRANCH_W), lambda b: (b, 0)),
                   pl.BlockSpec((t, BRANCH_W), lambda b: (b, 0))),
        compiler_params=_cparams(("parallel",)),
        name="ctx_attn",
    )(sink, p, p, p, p)


NA_ROWS = DEC_SEQ // GRID_W
NA_KEYS = NA_WR * GRID_W


def _lat_na_kernel(q_ref, k_ref, v_ref, ck_ref, cv_ref, bias_ref, o_ref):
    scale = HEAD_DIM ** -0.5
    r = pl.program_id(1)
    rs = jnp.clip(r - NA_WR // 2, 0, NA_ROWS - NA_WR)
    start = pl.multiple_of(rs * GRID_W, GRID_W)
    kw = k_ref[pl.ds(start, NA_KEYS), :]
    vw = v_ref[pl.ds(start, NA_KEYS), :]
    for h in range(NA_HEADS):
        sl = slice(h * 64, (h + 1) * 64)
        q = q_ref[:, sl]
        s_loc = _dot_nt(q, kw[:, sl]) * scale + bias_ref[h]
        s_ctx = _dot_nt(q, ck_ref[:, sl]) * scale
        m = jnp.maximum(jnp.max(s_loc, axis=1, keepdims=True), jnp.max(s_ctx, axis=1, keepdims=True))
        p_loc = jnp.exp(s_loc - m)
        p_ctx = jnp.exp(s_ctx - m)
        l = jnp.sum(p_loc, axis=1, keepdims=True) + jnp.sum(p_ctx, axis=1, keepdims=True)
        o_ref[:, sl] = (_dot(p_loc, vw[:, sl]) + _dot(p_ctx, cv_ref[:, sl])) / l


def _lat_na(p, ck, cv, bias):
    qb0 = N_CTX // GRID_W
    kb0 = N_CTX // DEC_SEQ
    return pl.pallas_call(
        _lat_na_kernel,
        out_shape=jax.ShapeDtypeStruct((N_LAT, BRANCH_W), F32),
        grid=(DEC_BATCH, NA_ROWS),
        in_specs=[
            pl.BlockSpec((GRID_W, 256), lambda b, r: (qb0 + b * NA_ROWS + r, 0)),
            pl.BlockSpec((DEC_SEQ, 256), lambda b, r: (kb0 + b, 1)),
            pl.BlockSpec((DEC_SEQ, 256), lambda b, r: (kb0 + b, 2)),
            pl.BlockSpec((None, PAST_LEN, 256), lambda b, r: (b, 0, 0)),
            pl.BlockSpec((None, PAST_LEN, 256), lambda b, r: (b, 0, 0)),
            pl.BlockSpec((NA_HEADS, None, GRID_W, NA_KEYS), lambda b, r: (0, r, 0, 0)),
        ],
        out_specs=pl.BlockSpec((GRID_W, BRANCH_W), lambda b, r: (b * NA_ROWS + r, 0)),
        compiler_params=_cparams(("parallel", "arbitrary")),
        name="lat_na",
    )(p, p, p, ck, cv, bias)


def _na_bias_table(rpb):
    r = np.arange(NA_ROWS)
    rs = np.clip(r - NA_WR // 2, 0, NA_ROWS - NA_WR)
    dr = rs[:, None] + np.arange(NA_WR)[None, :] - r[:, None] + NA_WR - 1
    qc = np.arange(GRID_W)[:, None]
    kc = np.arange(GRID_W)[None, :]
    cs = np.clip(qc - NA_WC // 2, 0, GRID_W - NA_WC)
    ok = (kc >= cs) & (kc < cs + NA_WC)
    dc = np.clip(kc - qc + NA_WC - 1, 0, 2 * NA_WC - 2)
    oh_r = (dr.reshape(-1)[:, None] == np.arange(2 * NA_WR - 1)[None, :]).astype(np.float32)
    oh_c = (np.arange(2 * NA_WC - 1)[:, None] == dc.reshape(-1)[None, :]).astype(np.float32)
    b = jnp.einsum('pd,hdc->hpc', jnp.asarray(oh_r), rpb.astype(F32), precision=HIGHEST)
    b = jnp.einsum('hpc,cq->hpq', b, jnp.asarray(oh_c), precision=HIGHEST)
    b = b.reshape(NA_HEADS, NA_ROWS, NA_WR, GRID_W, GRID_W).transpose(0, 1, 3, 2, 4)
    b = jnp.where(jnp.asarray(ok)[None, None, :, None, :], b, -jnp.inf)
    return b.reshape(NA_HEADS, NA_ROWS, GRID_W, NA_KEYS)


SW_NB = DEC_SEQ // SW_BLOCK
SW_KEYS = 3 * SW_BLOCK


def _rope(x, cos, sin):
    w = x.shape[1]
    lane = lax.broadcasted_iota(jnp.int32, x.shape, 1)
    first = (lane % 32) < 16
    swapped = jnp.where(first, pltpu.roll(x, w - 16, 1), pltpu.roll(x, 16, 1))
    return x * cos + swapped * sin


def _lat_swa_kernel(sink_ref, q_ref, k_ref, v_ref, ck_ref, cv_ref, cos_ref, sin_ref, o_ref):
    scale = HEAD_DIM ** -0.5
    n = pl.program_id(1)
    q0 = pl.multiple_of(n * SW_BLOCK, SW_BLOCK)
    start = pl.multiple_of(jnp.clip((n - 1) * SW_BLOCK, 0, DEC_SEQ - SW_KEYS), SW_BLOCK)
    q = _rope(q_ref[...], cos_ref[pl.ds(q0, SW_BLOCK), :], sin_ref[pl.ds(q0, SW_BLOCK), :])
    kw = _rope(k_ref[pl.ds(start, SW_KEYS), :], cos_ref[pl.ds(start, SW_KEYS), 0:128],
               sin_ref[pl.ds(start, SW_KEYS), 0:128])
    vw = v_ref[pl.ds(start, SW_KEYS), :]
    qpos = q0 + lax.broadcasted_iota(jnp.int32, (SW_BLOCK, SW_KEYS), 0)
    kpos = start + lax.broadcasted_iota(jnp.int32, (SW_BLOCK, SW_KEYS), 1)
    ok = jnp.abs(qpos - kpos) <= SW_WINDOW
    for h in range(SW_HEADS):
        kh = h // (SW_HEADS // SW_KV_HEADS)
        ksl = slice(kh * 64, (kh + 1) * 64)
        qh = q[:, h * 64:(h + 1) * 64]
        snk = sink_ref[h]
        s_loc = jnp.where(ok, _dot_nt(qh, kw[:, ksl]) * scale, -jnp.inf)
        s_ctx = _dot_nt(qh, ck_ref[:, ksl]) * scale
        m = jnp.maximum(jnp.maximum(jnp.max(s_loc, axis=1, keepdims=True),
                                    jnp.max(s_ctx, axis=1, keepdims=True)), snk)
        p_loc = jnp.exp(s_loc - m)
        p_ctx = jnp.exp(s_ctx - m)
        l = (jnp.sum(p_loc, axis=1, keepdims=True) + jnp.sum(p_ctx, axis=1, keepdims=True)
             + jnp.exp(snk - m))
        o_ref[:, h * 64:(h + 1) * 64] = (_dot(p_loc, vw[:, ksl]) + _dot(p_ctx, cv_ref[:, ksl])) / l


def _lat_swa(p, sink, ck, cv, cos, sin):
    qb0 = N_CTX // SW_BLOCK
    kb0 = N_CTX // DEC_SEQ
    return pl.pallas_call(
        _lat_swa_kernel,
        out_shape=jax.ShapeDtypeStruct((N_LAT, BRANCH_W), F32),
        grid=(DEC_BATCH, SW_NB),
        in_specs=[
            pl.BlockSpec(memory_space=pltpu.SMEM),
            pl.BlockSpec((SW_BLOCK, 256), lambda b, n: (qb0 + b * SW_NB + n, P_SWQ // 256)),
            pl.BlockSpec((DEC_SEQ, 128), lambda b, n: (kb0 + b, P_SWK // 128)),
            pl.BlockSpec((DEC_SEQ, 128), lambda b, n: (kb0 + b, P_SWV // 128)),
            pl.BlockSpec((None, PAST_LEN, 128), lambda b, n: (b, 0, 0)),
            pl.BlockSpec((None, PAST_LEN, 128), lambda b, n: (b, 0, 0)),
            pl.BlockSpec((DEC_SEQ, 256), lambda b, n: (0, 0)),
            pl.BlockSpec((DEC_SEQ, 256), lambda b, n: (0, 0)),
        ],
        out_specs=pl.BlockSpec((SW_BLOCK, BRANCH_W), lambda b, n: (b * SW_NB + n, 0)),
        compiler_params=_cparams(("parallel", "arbitrary")),
        name="lat_swa",
    )(sink, p, p, p, ck, cv, cos, sin)


def _rope_tables():
    t = np.arange(DEC_SEQ)
    half = 16
    freqs = ROPE_BASE ** (-np.arange(half, dtype=np.float64) / half)
    ang_r = (t // GRID_W)[:, None] * freqs[None, :]
    ang_c = (t % GRID_W)[:, None] * freqs[None, :]
    cos = np.concatenate([np.cos(ang_r), np.cos(ang_r), np.cos(ang_c), np.cos(ang_c)], axis=1)
    sin = np.concatenate([-np.sin(ang_r), np.sin(ang_r), -np.sin(ang_c), np.sin(ang_c)], axis=1)
    return (jnp.asarray(np.tile(cos, (1, 4)), F32), jnp.asarray(np.tile(sin, (1, 4)), F32))


def _dir_masks(d, n):
    row = lax.broadcasted_iota(jnp.int32, (n, n), 0)
    col = lax.broadcasted_iota(jnp.int32, (n, n), 1)
    u = (col - row) * (1 - 2 * d)
    return row, col, u


def _seq_specs(nblk, b0_blk):
    def blk(b, d, j):
        return b0_blk + b * nblk + jnp.where(d == 0, j, nblk - 1 - j)
    return blk


def _mlstm_kernel(qkv_ref, gt_ref, gc_ref, c0_ref, m0_ref, br_ref, bc_ref,
                  h_ref, cfin_ref, mfin_ref, c_s, m_s, *, nblk):
    n = SEQ_BLK
    d = pl.program_id(1)
    j = pl.program_id(2)

    @pl.when(j == 0)
    def _():
        c_s[...] = c0_ref[...]
        m_s[...] = m0_ref[...]

    _, col, u = _dir_masks(d, n)
    incl = u <= 0
    tri_c = jnp.where(incl, 1.0, 0.0).astype(F32)
    tri_r = jnp.where(u >= 0, 1.0, 0.0).astype(F32)
    end_lane = jnp.where(d == 0, n - 1, 0)

    g_r = gt_ref[0:8, :] + br_ref[...]
    f_r = _log_sigmoid(g_r)
    bcum_r = _dot_f32(f_r, tri_r)
    g_c = gc_ref[...] + bc_ref[...]
    f_c = _log_sigmoid(g_c)
    bcum_c = _dot_f32(tri_c, f_c)

    lane_r = lax.broadcasted_iota(jnp.int32, (1, n), 1)
    ones_col = jnp.where(lax.broadcasted_iota(jnp.int32, (n, 64), 1) == 0, 1.0, 0.0).astype(F32)

    hs = range(ML_HEADS)
    q = [qkv_ref[:, h * 64:(h + 1) * 64] for h in hs]
    k = [qkv_ref[:, 256 + h * 64:256 + (h + 1) * 64] * (64 ** -0.5) for h in hs]
    vp = [jnp.concatenate([qkv_ref[:, 512 + h * 64:512 + (h + 1) * 64], ones_col], axis=1)
          for h in hs]
    b_r = [bcum_r[4 + h:5 + h, :] for h in hs]
    i_c = [g_c[:, h:h + 1] for h in hs]
    b_c = [bcum_c[:, 4 + h:5 + h] for h in hs]
    m_prev = [m_s[h][:, 0:1] for h in hs]
    cp = [c_s[h] for h in hs]
    r_row = [g_r[h:h + 1, :] - b_r[h] for h in hs]

    qk = [_dot_nt(q[h], k[h]) for h in hs]
    qc = [_dot(q[h], cp[h]) for h in hs]
    inter = [b_c[h] + m_prev[h] for h in hs]
    dmat = [jnp.where(incl, b_c[h] + r_row[h], -jnp.inf) for h in hs]
    mt = [jnp.maximum(inter[h], jnp.max(dmat[h], axis=1, keepdims=True)) for h in hs]
    s = [qk[h] * jnp.exp(dmat[h] - mt[h]) for h in hs]
    numden = [_dot(s[h], vp[h]) + jnp.exp(inter[h] - mt[h]) * qc[h] for h in hs]
    for h in hs:
        num = numden[h][:, 0:64]
        den = numden[h][:, 64:65]
        h_ref[:, h * 64:(h + 1) * 64] = num / jnp.maximum(jnp.abs(den), jnp.exp(-mt[h]))

    b_last = [jnp.sum(jnp.where(lane_r == end_lane, b_r[h], 0.0), axis=1, keepdims=True)
              for h in hs]
    m_new = [jnp.maximum(b_last[h] + m_prev[h], jnp.max(b_last[h] + r_row[h], axis=1, keepdims=True))
             for h in hs]
    kw = [k[h] * jnp.exp(b_last[h] + (i_c[h] - b_c[h]) - m_new[h]) for h in hs]
    upd = [_dot_tn(kw[h], vp[h]) for h in hs]
    for h in hs:
        c_s[h] = jnp.exp(b_last[h] + m_prev[h] - m_new[h]) * cp[h] + upd[h]
        m_s[h] = jnp.broadcast_to(m_new[h], (1, 128))

    @pl.when(j == nblk - 1)
    def _():
        cfin_ref[...] = c_s[...]
        mfin_ref[...] = m_s[...]


def _mlstm(p, gt, c0p, m0p, br, bc, nb, t, tok0):
    nblk = t // SEQ_BLK
    blk = _seq_specs(nblk, tok0 // SEQ_BLK)
    blk_out = _seq_specs(nblk, 0)
    ntok = nb * t
    return pl.pallas_call(
        functools.partial(_mlstm_kernel, nblk=nblk),
        out_shape=(jax.ShapeDtypeStruct((2, ntok, 256), F32),
                   jax.ShapeDtypeStruct((nb, 2, ML_HEADS, 64, 128), F32),
                   jax.ShapeDtypeStruct((nb, 2, ML_HEADS, 1, 128), F32)),
        grid=(nb, 2, nblk),
        in_specs=[
            pl.BlockSpec((SEQ_BLK, 768), lambda b, d, j: (blk(b, d, j), P_ML // 768)),
            pl.BlockSpec((None, 16, SEQ_BLK), lambda b, d, j: (d, 0, blk(b, d, j))),
            pl.BlockSpec((SEQ_BLK, 128), lambda b, d, j: (blk(b, d, j), P_GATE // 128 + d)),
            pl.BlockSpec((None, None, ML_HEADS, 64, 128), lambda b, d, j: (b, d, 0, 0, 0)),
            pl.BlockSpec((None, None, ML_HEADS, 1, 128), lambda b, d, j: (b, d, 0, 0, 0)),
            pl.BlockSpec((None, 8, 1), lambda b, d, j: (d, 0, 0)),
            pl.BlockSpec((None, 1, 128), lambda b, d, j: (d, 0, 0)),
        ],
        out_specs=(pl.BlockSpec((None, SEQ_BLK, 256), lambda b, d, j: (d, blk_out(b, d, j), 0)),
                   pl.BlockSpec((None, None, ML_HEADS, 64, 128), lambda b, d, j: (b, d, 0, 0, 0)),
                   pl.BlockSpec((None, None, ML_HEADS, 1, 128), lambda b, d, j: (b, d, 0, 0, 0))),
        scratch_shapes=[pltpu.VMEM((ML_HEADS, 64, 128), F32), pltpu.VMEM((ML_HEADS, 1, 128), F32)],
        compiler_params=_cparams(("parallel", "arbitrary", "arbitrary")),
        name="mlstm",
    )(p, gt, p, c0p, m0p, br, bc)


def _gdn_pre_kernel(x_ref, w_ref, o_ref, pad_s, *, t):
    pad_s[0:8, :] = jnp.zeros((8, 768), F32)
    pad_s[t + 8:t + 16, :] = jnp.zeros((8, 768), F32)
    pad_s[8:t + 8, :] = x_ref[...]
    y = jnp.zeros((t, 768), F32)
    for jj in range(DN_CONV):
        off = 8 + jj - DN_CONV // 2
        y = y + pad_s[off:off + t, :] * w_ref[jj:jj + 1, :]
    a = y * _sigmoid(y)
    for h in range(DN_HEADS):
        for part, mul in ((0, 64 ** -0.5), (256, 1.0)):
            sl = slice(part + h * 64, part + (h + 1) * 64)
            z = a[:, sl]
            o_ref[:, sl] = z * lax.rsqrt(jnp.sum(z * z, axis=1, keepdims=True) + NORM_EPS) * mul
    o_ref[:, 512:768] = a[:, 512:768]


def _gdn_pre(p, conv_w, nb, t, tok0):
    b0 = tok0 // t
    return pl.pallas_call(
        functools.partial(_gdn_pre_kernel, t=t),
        out_shape=jax.ShapeDtypeStruct((nb * t, 768), F32),
        grid=(nb,),
        in_specs=[
            pl.BlockSpec((t, 768), lambda b: (b0 + b, P_DN // 768)),
            pl.BlockSpec((8, 768), lambda b: (0, 0)),
        ],
        out_specs=pl.BlockSpec((t, 768), lambda b: (b, 0)),
        scratch_shapes=[pltpu.VMEM((t + 16, 768), F32)],
        compiler_params=_cparams(("parallel",)),
        name="gdn_pre",
    )(p, conv_w)


def _gdn_kernel(qkv_ref, gt_ref, gc_ref, s0_ref, pr_ref, pc_ref, o_ref, sfin_ref, s_s, *, nblk):
    n = SEQ_BLK
    d = pl.program_id(1)
    j = pl.program_id(2)

    @pl.when(j == 0)
    def _():
        s_s[...] = s0_ref[...]

    row, col, u = _dir_masks(d, n)
    incl = u <= 0
    strict = u < 0
    tri_c = jnp.where(incl, 1.0, 0.0).astype(F32)
    tri_r = jnp.where(u >= 0, 1.0, 0.0).astype(F32)
    end_lane = jnp.where(d == 0, n - 1, 0)
    eye = jnp.where(u == 0, 1.0, 0.0).astype(F32)

    a_r = gt_ref[8:16, :]
    g_r = -jnp.exp(pr_ref[0:8, :]) * _softplus(a_r + pr_ref[8:16, :])
    gcum_r = _dot_f32(g_r, tri_r)
    a_c = gc_ref[...]
    g_c = -jnp.exp(pc_ref[1:2, :]) * _softplus(a_c + pc_ref[0:1, :])
    gcum_c = _dot_f32(tri_c, g_c)
    beta_c = _sigmoid(a_c)
    lane_r = lax.broadcasted_iota(jnp.int32, (1, n), 1)

    hs = range(DN_HEADS)
    qn = [qkv_ref[:, h * 64:(h + 1) * 64] for h in hs]
    kn = [qkv_ref[:, 256 + h * 64:256 + (h + 1) * 64] for h in hs]
    vv = [qkv_ref[:, 512 + h * 64:512 + (h + 1) * 64] for h in hs]
    gc_row = [gcum_r[h:h + 1, :] for h in hs]
    gc_col = [gcum_c[:, 8 + h:9 + h] for h in hs]
    beta = [beta_c[:, 12 + h:13 + h] for h in hs]
    s_prev = [s_s[h] for h in hs]

    decay = [jnp.exp(jnp.where(incl, gc_col[h] - gc_row[h], -jnp.inf)) for h in hs]
    kb = [kn[h] * beta[h] for h in hs]
    gram = [_dot_nt(kb[h], kn[h]) for h in hs]
    a_mat = [jnp.where(strict, gram[h] * decay[h], 0.0) for h in hs]
    e_col = [jnp.exp(gc_col[h]) for h in hs]
    rhs = [jnp.concatenate([vv[h] * beta[h], kb[h] * e_col[h]], axis=1) for h in hs]

    same2 = (row >> 1) == (col >> 1)
    tinv = [eye - jnp.where(same2, a_mat[h], 0.0) for h in hs]
    for lb in range(1, 8):
        inner = (row >> lb) == (col >> lb)
        outer = (row >> (lb + 1)) == (col >> (lb + 1))
        off = jnp.logical_and(outer, jnp.logical_not(inner))
        m1 = [_dot(jnp.where(off, a_mat[h], 0.0), tinv[h]) for h in hs]
        tinv = [tinv[h] - _dot(tinv[h], m1[h]) for h in hs]
    sol = [_dot(tinv[h], rhs[h]) for h in hs]
    qk = [_dot_nt(qn[h], kn[h]) * decay[h] for h in hs]
    g_last = [jnp.sum(jnp.where(lane_r == end_lane, gc_row[h], 0.0), axis=1, keepdims=True)
              for h in hs]
    v_new = [sol[h][:, 0:64] - _dot(sol[h][:, 64:128], s_prev[h]) for h in hs]
    for h in hs:
        o_ref[:, h * 64:(h + 1) * 64] = _dot(qn[h] * e_col[h], s_prev[h]) + _dot(qk[h], v_new[h])
    for h in hs:
        k_dec = kn[h] * jnp.exp(g_last[h] - gc_col[h])
        s_s[h] = jnp.exp(g_last[h]) * s_prev[h] + _dot_tn(k_dec, v_new[h])

    @pl.when(j == nblk - 1)
    def _():
        sfin_ref[...] = s_s[...]


def _gdn(qkv, p, gt, s0, pr, pc, nb, t, tok0):
    nblk = t // SEQ_BLK
    blk = _seq_specs(nblk, tok0 // SEQ_BLK)
    blk0 = _seq_specs(nblk, 0)
    ntok = nb * t
    return pl.pallas_call(
        functools.partial(_gdn_kernel, nblk=nblk),
        out_shape=(jax.ShapeDtypeStruct((2, ntok, 256), F32),
                   jax.ShapeDtypeStruct((nb, 2, DN_HEADS, 64, 64), F32)),
        grid=(nb, 2, nblk),
        in_specs=[
            pl.BlockSpec((SEQ_BLK, 768), lambda b, d, j: (blk0(b, d, j), 0)),
            pl.BlockSpec((None, 16, SEQ_BLK), lambda b, d, j: (d, 0, blk(b, d, j))),
            pl.BlockSpec((SEQ_BLK, 128), lambda b, d, j: (blk(b, d, j), P_GATE // 128 + d)),
            pl.BlockSpec((None, None, DN_HEADS, 64, 64), lambda b, d, j: (b, d, 0, 0, 0)),
            pl.BlockSpec((None, 16, 1), lambda b, d, j: (d, 0, 0)),
            pl.BlockSpec((None, 2, 128), lambda b, d, j: (d, 0, 0)),
        ],
        out_specs=(pl.BlockSpec((None, SEQ_BLK, 256), lambda b, d, j: (d, blk0(b, d, j), 0)),
                   pl.BlockSpec((None, None, DN_HEADS, 64, 64), lambda b, d, j: (b, d, 0, 0, 0))),
        scratch_shapes=[pltpu.VMEM((DN_HEADS, 64, 64), F32)],
        compiler_params=_cparams(("parallel", "arbitrary", "arbitrary")),
        name="gdn",
    )(qkv, gt, p, s0, pr, pc)


MG_TM = 512


def _merge_kernel(x_ref, sh_ref, sc_ref, gt_ref, ona_ref, osw_ref, hf_ref, hb_ref, mlo_ref,
                  df_ref, db_ref, dng_ref, mlg_ref, dngain_ref, wmg_ref, wb_ref, wo_ref,
                  lng_ref, lnb_ref, o_ref):
    x = x_ref[...]
    xm = (x * (1.0 + sc_ref[...]) + sh_ref[...]).astype(BF16)
    hh = hf_ref[...] + hb_ref[...]
    dd = df_ref[...] + db_ref[...]
    ml_parts = []
    dn_parts = []
    for h in range(ML_HEADS):
        sl = slice(h * 64, (h + 1) * 64)
        z = hh[:, sl]
        mu = jnp.mean(z, axis=1, keepdims=True)
        zc = z - mu
        var = jnp.mean(zc * zc, axis=1, keepdims=True)
        ml_parts.append(zc * lax.rsqrt(var + LN_EPS))
        z = dd[:, sl]
        dn_parts.append(z * lax.rsqrt(jnp.mean(z * z, axis=1, keepdims=True) + NORM_EPS))
    o_ml = jnp.concatenate(ml_parts, axis=1) * mlg_ref[...] * _sigmoid(mlo_ref[...])
    g_pre = dng_ref[...]
    o_dn = jnp.concatenate(dn_parts, axis=1) * dngain_ref[...] * (g_pre * _sigmoid(g_pre))
    acc = jnp.zeros((x.shape[0], D_MODEL), F32)
    for nbr, o_n in enumerate((ona_ref[...], o_ml, o_dn, osw_ref[...])):
        y = _dot(o_n, wb_ref[nbr])
        gate = _sigmoid(jnp.dot(xm, wmg_ref[:, nbr * D_MODEL:(nbr + 1) * D_MODEL],
                                preferred_element_type=F32))
        acc = acc + gate * y
    mix = _dot(acc, wo_ref[...])
    o_ref[...] = _layer_norm(ALPHA * x + gt_ref[...] * mix, lng_ref[...], lnb_ref[...])


def _merge(x, mod_l, p, o_na, o_sw, hml, odn, ml_gain, dn_gain, wmg, wb, wo, ln_g, ln_b):
    tm = MG_TM
    row = lambda i: (i, 0)
    const2 = lambda i: (0, 0)
    return pl.pallas_call(
        _merge_kernel,
        out_shape=jax.ShapeDtypeStruct((N_TOK, D_MODEL), F32),
        grid=(N_TOK // tm,),
        in_specs=[
            pl.BlockSpec((tm, D_MODEL), row),
            _mod_spec(0, tm), _mod_spec(1, tm), _mod_spec(2, tm),
            pl.BlockSpec((tm, 256), row),
            pl.BlockSpec((tm, 256), row),
            pl.BlockSpec((None, tm, 256), lambda i: (0, i, 0)),
            pl.BlockSpec((None, tm, 256), lambda i: (1, i, 0)),
            pl.BlockSpec((tm, 256), lambda i: (i, P_MLO // 256)),
            pl.BlockSpec((None, tm, 256), lambda i: (0, i, 0)),
            pl.BlockSpec((None, tm, 256), lambda i: (1, i, 0)),
            pl.BlockSpec((tm, 256), lambda i: (i, P_DNG // 256)),
            pl.BlockSpec((1, 256), const2),
            pl.BlockSpec((1, 256), const2),
            pl.BlockSpec((D_MODEL, N_BRANCH * D_MODEL), const2),
            pl.BlockSpec((N_BRANCH, BRANCH_W, D_MODEL), lambda i: (0, 0, 0)),
            pl.BlockSpec((D_MODEL, D_MODEL), const2),
            pl.BlockSpec((1, D_MODEL), const2),
            pl.BlockSpec((1, D_MODEL), const2),
        ],
        out_specs=pl.BlockSpec((tm, D_MODEL), row),
        compiler_params=_cparams(("parallel",)),
        name="merge",
    )(x, mod_l, mod_l, mod_l, o_na, o_sw, hml, hml, p, odn, odn, p, ml_gain, dn_gain,
      wmg, wb, wo, ln_g, ln_b)


FF_TM = 512
FF_TH = 1408
FF_NK = FFN_DIM // FF_TH


def _ffn_kernel(x_ref, sh_ref, sc_ref, gt_ref, wg_ref, wu_ref, wo_ref, lng_ref, lnb_ref, o_ref,
                xm_s, acc_s):
    k = pl.program_id(1)

    @pl.when(k == 0)
    def _():
        xm_s[...] = (x_ref[...] * (1.0 + sc_ref[...]) + sh_ref[...]).astype(BF16)
        acc_s[...] = jnp.zeros_like(acc_s)

    xm = xm_s[...]
    hg = jnp.dot(xm, wg_ref[...], preferred_element_type=F32)
    hu = jnp.dot(xm, wu_ref[...], preferred_element_type=F32)
    a = hg * _sigmoid(hg) * hu
    acc_s[...] += _dot(a, wo_ref[...])

    @pl.when(k == FF_NK - 1)
    def _():
        z = ALPHA * x_ref[...] + gt_ref[...] * acc_s[...]
        o_ref[...] = _layer_norm(z, lng_ref[...], lnb_ref[...])


def _ffn(x, mod_l, w_in, w_out, ln_g, ln_b):
    tm = FF_TM
    return pl.pallas_call(
        _ffn_kernel,
        out_shape=jax.ShapeDtypeStruct((N_TOK, D_MODEL), F32),
        grid=(N_TOK // tm, FF_NK),
        in_specs=[
            pl.BlockSpec((tm, D_MODEL), lambda i, k: (i, 0)),
            _mod_spec(3, tm), _mod_spec(4, tm), _mod_spec(5, tm),
            pl.BlockSpec((D_MODEL, FF_TH), lambda i, k: (0, k)),
            pl.BlockSpec((D_MODEL, FF_TH), lambda i, k: (0, FF_NK + k)),
            pl.BlockSpec((FF_TH, D_MODEL), lambda i, k: (k, 0)),
            pl.BlockSpec((1, D_MODEL), lambda i, k: (0, 0)),
            pl.BlockSpec((1, D_MODEL), lambda i, k: (0, 0)),
        ],
        out_specs=pl.BlockSpec((tm, D_MODEL), lambda i, k: (i, 0)),
        scratch_shapes=[pltpu.VMEM((tm, D_MODEL), BF16), pltpu.VMEM((tm, D_MODEL), F32)],
        compiler_params=_cparams(("parallel", "arbitrary")),
        name="ffn",
    )(x, mod_l, mod_l, mod_l, w_in, w_in, w_out, ln_g, ln_b)


def _router_kernel(x_ref, sh_ref, sc_ref, wr_ref, br_ref, g_ref, xm_ref):
    xm = x_ref[...] * (1.0 + sc_ref[...]) + sh_ref[...]
    logits = _dot_f32(xm, wr_ref[...]) + br_ref[...]
    lane = lax.broadcasted_iota(jnp.int32, logits.shape, 1)
    neg = jnp.where(lane < N_EXPERTS, logits, -jnp.inf)
    v1 = jnp.max(neg, axis=1, keepdims=True)
    i1 = jnp.min(jnp.where(neg == v1, lane, 128), axis=1, keepdims=True)
    rest = jnp.where(lane == i1, -jnp.inf, neg)
    v2 = jnp.max(rest, axis=1, keepdims=True)
    i2 = jnp.min(jnp.where(rest == v2, lane, 128), axis=1, keepdims=True)
    e2 = jnp.exp(v2 - v1)
    p1 = 1.0 / (1.0 + e2)
    p2 = e2 / (1.0 + e2)
    g_ref[...] = (jnp.where(lane == 0, i1.astype(F32), 0.0) + jnp.where(lane == 1, i2.astype(F32), 0.0)
                  + jnp.where(lane == 2, p1, 0.0) + jnp.where(lane == 3, p2, 0.0))
    xm_ref[...] = xm


def _router(x, mod_l, wr, br):
    tm = 512
    return pl.pallas_call(
        _router_kernel,
        out_shape=(jax.ShapeDtypeStruct((N_TOK, 128), F32),
                   jax.ShapeDtypeStruct((N_TOK, D_MODEL), F32)),
        grid=(N_TOK // tm,),
        in_specs=[
            pl.BlockSpec((tm, D_MODEL), lambda i: (i, 0)),
            _mod_spec(3, tm), _mod_spec(4, tm),
            pl.BlockSpec((D_MODEL, 128), lambda i: (0, 0)),
            pl.BlockSpec((1, 128), lambda i: (0, 0)),
        ],
        out_specs=(pl.BlockSpec((tm, 128), lambda i: (i, 0)),
                   pl.BlockSpec((tm, D_MODEL), lambda i: (i, 0))),
        compiler_params=_cparams(("parallel",)),
        name="router",
    )(x, mod_l, mod_l, wr, br)


GATHER_ROWS = 256


def _gather_kernel(idx_ref, src_ref, o_ref, sem):
    base = pl.program_id(0) * GATHER_ROWS

    def _copy(r, src_row):
        return pltpu.make_async_copy(src_ref.at[pl.ds(src_row, 1)], o_ref.at[pl.ds(r, 1)], sem)

    def _start(r, carry):
        _copy(r, idx_ref[base + r]).start()
        return carry

    def _wait(r, carry):
        _copy(r, 0).wait()
        return carry

    lax.fori_loop(0, GATHER_ROWS, _start, 0)
    lax.fori_loop(0, GATHER_ROWS, _wait, 0)


def _row_gather(src, idx):
    m = idx.shape[0]
    return pl.pallas_call(
        _gather_kernel,
        out_shape=jax.ShapeDtypeStruct((m, src.shape[1]), src.dtype),
        grid_spec=pltpu.PrefetchScalarGridSpec(
            num_scalar_prefetch=1,
            grid=(m // GATHER_ROWS,),
            in_specs=[pl.BlockSpec(memory_space=pl.ANY)],
            out_specs=pl.BlockSpec((GATHER_ROWS, src.shape[1]), lambda i, idx_ref: (i, 0)),
            scratch_shapes=[pltpu.SemaphoreType.DMA(())],
        ),
        compiler_params=_cparams(("arbitrary",)),
        name="row_gather",
    )(idx, src)


MOE_TM = 512
MOE_NT = 2 * N_TOK // MOE_TM + N_EXPERTS


def _route_plan(aux):
    e = jnp.concatenate([aux[:, 0], aux[:, 1]]).astype(jnp.int32)
    tok = jnp.tile(jnp.arange(N_TOK, dtype=jnp.int32), 2)
    onehot = (e[:, None] == jnp.arange(N_EXPERTS, dtype=jnp.int32)[None, :]).astype(jnp.int32)
    csum = jnp.cumsum(onehot, axis=0)
    rank = jnp.sum((csum - onehot) * onehot, axis=1)
    counts = csum[-1]
    tiles_per = (counts + MOE_TM - 1) // MOE_TM
    tile_end = jnp.cumsum(tiles_per)
    tile_start = tile_end - tiles_per
    dest = jnp.sum(onehot * tile_start[None, :], axis=1) * MOE_TM + rank
    row_src = jnp.zeros((MOE_NT * MOE_TM,), jnp.int32).at[dest].set(tok)
    n_used = tile_end[-1]
    tiles = jnp.minimum(jnp.arange(MOE_NT, dtype=jnp.int32), n_used - 1)
    tile_expert = jnp.sum((tiles[:, None] >= tile_end[None, :]).astype(jnp.int32), axis=1)
    pos = jnp.stack([dest[:N_TOK], dest[N_TOK:]], axis=1).reshape(2 * N_TOK)
    return row_src, pos, tile_expert.astype(jnp.int32), n_used.reshape(1).astype(jnp.int32)


def _moe_ffn_kernel(te_ref, nu_ref, x_ref, wg_ref, wu_ref, wo_ref, o_ref, xm_s, acc_s):
    i = pl.program_id(0)
    k = pl.program_id(1)
    used = i < nu_ref[0]

    @pl.when(jnp.logical_and(used, k == 0))
    def _():
        xm_s[...] = x_ref[...].astype(BF16)
        acc_s[...] = jnp.zeros_like(acc_s)

    @pl.when(used)
    def _():
        xm = xm_s[...]
        hg = jnp.dot(xm, wg_ref[...], preferred_element_type=F32)
        hu = jnp.dot(xm, wu_ref[...], preferred_element_type=F32)
        a = hg * _sigmoid(hg) * hu
        acc_s[...] += _dot(a, wo_ref[...])

    @pl.when(k == FF_NK - 1)
    def _():
        o_ref[...] = jnp.where(used, acc_s[...], 0.0)


def _moe_ffn(xg, tile_expert, n_used, w_in, w_out):
    tm = MOE_TM

    def kk(i, k, nu):
        return jnp.where(i < nu[0], k, FF_NK - 1)

    return pl.pallas_call(
        _moe_ffn_kernel,
        out_shape=jax.ShapeDtypeStruct((MOE_NT * tm, D_MODEL), F32),
        grid_spec=pltpu.PrefetchScalarGridSpec(
            num_scalar_prefetch=2,
            grid=(MOE_NT, FF_NK),
            in_specs=[
                pl.BlockSpec((tm, D_MODEL), lambda i, k, te, nu: (i, 0)),
                pl.BlockSpec((None, D_MODEL, FF_TH), lambda i, k, te, nu: (te[i], 0, kk(i, k, nu))),
                pl.BlockSpec((None, D_MODEL, FF_TH),
                             lambda i, k, te, nu: (te[i], 0, FF_NK + kk(i, k, nu))),
                pl.BlockSpec((None, FF_TH, D_MODEL), lambda i, k, te, nu: (te[i], kk(i, k, nu), 0)),
            ],
            out_specs=pl.BlockSpec((tm, D_MODEL), lambda i, k, te, nu: (i, 0)),
            scratch_shapes=[pltpu.VMEM((tm, D_MODEL), BF16), pltpu.VMEM((tm, D_MODEL), F32)],
        ),
        compiler_params=_cparams(("arbitrary", "arbitrary")),
        name="moe_ffn",
    )(tile_expert, n_used, xg, w_in, w_in, w_out)


def _moe_finish_kernel(x_ref, gt_ref, aux_ref, y_ref, lng_ref, lnb_ref, o_ref):
    aux = aux_ref[...]
    ff = aux[:, 2:3] * y_ref[:, 0:D_MODEL] + aux[:, 3:4] * y_ref[:, D_MODEL:2 * D_MODEL]
    z = ALPHA * x_ref[...] + gt_ref[...] * ff
    o_ref[...] = _layer_norm(z, lng_ref[...], lnb_ref[...])


def _moe_finish(x, mod_l, aux, yg, ln_g, ln_b):
    tm = 512
    return pl.pallas_call(
        _moe_finish_kernel,
        out_shape=jax.ShapeDtypeStruct((N_TOK, D_MODEL), F32),
        grid=(N_TOK // tm,),
        in_specs=[
            pl.BlockSpec((tm, D_MODEL), lambda i: (i, 0)),
            _mod_spec(5, tm),
            pl.BlockSpec((tm, 128), lambda i: (i, 0)),
            pl.BlockSpec((tm, 2 * D_MODEL), lambda i: (i, 0)),
            pl.BlockSpec((1, D_MODEL), lambda i: (0, 0)),
            pl.BlockSpec((1, D_MODEL), lambda i: (0, 0)),
        ],
        out_specs=pl.BlockSpec((tm, D_MODEL), lambda i: (i, 0)),
        compiler_params=_cparams(("parallel",)),
        name="moe_finish",
    )(x, mod_l, aux, yg, ln_g, ln_b)


def _split_w_in(w):
    na = w[:, 0:768]
    ml = w[:, 768:1536]
    mlg = w[:, 1536:1552]
    mlo = w[:, 1552:1808]
    dn = w[:, 1808:2576]
    dna = w[:, 2576:2584]
    dnb = w[:, 2584:2592]
    dng = w[:, 2592:2848]
    sw = w[:, 2848:3360]
    mg = w[:, 3360:7456]
    zpad = jnp.zeros((D_MODEL, 112), w.dtype)
    gates = []
    for d in range(2):
        gates.append(jnp.concatenate([mlg[:, 4 * d:4 * d + 4], mlg[:, 8 + 4 * d:12 + 4 * d],
                                      dna[:, 4 * d:4 * d + 4], dnb[:, 4 * d:4 * d + 4]], axis=1))
    w1 = jnp.concatenate([na, ml, dn, mlo, dng, sw, gates[0], zpad, gates[1], zpad], axis=1)
    wt = jnp.concatenate(gates, axis=1).T
    return w1.astype(BF16), wt.astype(BF16), mg.astype(BF16)


def kernel(x_prompt, x_sample, c, cache_na_k, cache_na_v, cache_sw_k, cache_sw_v, state_ml_c, state_ml_n, state_ml_m, state_dn_s, c_ctx, ada_w, ada_b, w_in, na_rpb, ml_gate_b, ml_norm_g, dn_conv_w, dn_a_log, dn_dt_bias, dn_norm_g, sw_sink, w_branch, w_out, ln_g, ln_b, ffn_w_in, ffn_w_out, moe_router, moe_router_b, moe_w_in, moe_w_out):
    x = jnp.concatenate([x_prompt.reshape(N_CTX, D_MODEL), x_sample.reshape(N_LAT, D_MODEL)], axis=0)
    cvec = jnp.concatenate([c_ctx[None, :], c, jnp.zeros((3, D_MODEL), F32)], axis=0)
    mod = _ada(cvec, ada_w, ada_b).reshape(DEPTH, 8, 6, 1, D_MODEL)
    cos, sin = _rope_tables()

    zc_ctx = jnp.zeros((BATCH, 2, ML_HEADS, 64, 128), F32)
    zm_ctx = jnp.zeros((BATCH, 2, ML_HEADS, 1, 128), F32)
    zs_ctx = jnp.zeros((BATCH, 2, DN_HEADS, 64, 64), F32)
    pad63 = jnp.zeros((DEC_BATCH, 2, ML_HEADS, 64, 63), F32)

    new = [[] for _ in range(8)]
    for l in range(DEPTH):
        w1, wt, wmg = _split_w_in(w_in[l])
        mod_l = mod[l]
        p, gt = _inproj(x, mod_l, w1, wt)

        o_na_c, o_sw_c = _ctx_attn(p, sw_sink[l])
        ck = cache_na_k[:, l].reshape(DEC_BATCH, PAST_LEN, 256)
        cv = cache_na_v[:, l].reshape(DEC_BATCH, PAST_LEN, 256)
        o_na_l = _lat_na(p, ck, cv, _na_bias_table(na_rpb[l]))
        sk = cache_sw_k[:, l].reshape(DEC_BATCH, PAST_LEN, 128)
        sv = cache_sw_v[:, l].reshape(DEC_BATCH, PAST_LEN, 128)
        o_sw_l = _lat_swa(p, sw_sink[l], sk, sv, cos, sin)

        gb = ml_gate_b[l]
        br = jnp.stack([jnp.concatenate([gb[0], gb[2]]), jnp.concatenate([gb[1], gb[3]])])[:, :, None]
        bc = jnp.pad(br[:, :, 0], ((0, 0), (0, 120)))[:, None, :]
        h_c, cfin, mfin = _mlstm(p, gt, zc_ctx, zm_ctx, br, bc, BATCH, SEQ, 0)
        c0p = jnp.concatenate([state_ml_c[:, l], state_ml_n[:, l][..., None], pad63], axis=-1)
        m0p = jnp.broadcast_to(state_ml_m[:, l][..., None, None], (DEC_BATCH, 2, ML_HEADS, 1, 128))
        h_l, _, _ = _mlstm(p, gt, c0p, m0p, br, bc, DEC_BATCH, DEC_SEQ, N_CTX)

        conv_w = jnp.pad(dn_conv_w[l], ((0, 3), (0, 0)))
        z4 = jnp.zeros((2, 4), F32)
        pr = jnp.concatenate([dn_a_log[l], z4, dn_dt_bias[l], z4], axis=1)[:, :, None]
        pc = jnp.stack([jnp.pad(dn_dt_bias[l], ((0, 0), (8, 116))),
                        jnp.pad(dn_a_log[l], ((0, 0), (8, 116)))], axis=1)
        qkv_c = _gdn_pre(p, conv_w, BATCH, SEQ, 0)
        o_c, sfin = _gdn(qkv_c, p, gt, zs_ctx, pr, pc, BATCH, SEQ, 0)
        qkv_l = _gdn_pre(p, conv_w, DEC_BATCH, DEC_SEQ, N_CTX)
        o_l, _ = _gdn(qkv_l, p, gt, state_dn_s[:, l], pr, pc, DEC_BATCH, DEC_SEQ, N_CTX)

        o_na = jnp.concatenate([o_na_c, o_na_l], axis=0)
        o_sw = jnp.concatenate([o_sw_c, o_sw_l], axis=0)
        hml = jnp.concatenate([h_c, h_l], axis=1)
        odn = jnp.concatenate([o_c, o_l], axis=1)
        x = _merge(x, mod_l, p, o_na, o_sw, hml, odn, ml_norm_g[l][None, :],
                   jnp.tile(dn_norm_g[l], DN_HEADS)[None, :], wmg, w_branch[l].astype(BF16),
                   w_out[l].astype(BF16), ln_g[l, 0][None, :], ln_b[l, 0][None, :])

        jx = l // 2
        if l % 2 == 0:
            x = _ffn(x, mod_l, ffn_w_in[jx].astype(BF16), ffn_w_out[jx].astype(BF16),
                     ln_g[l, 1][None, :], ln_b[l, 1][None, :])
        else:
            wr = jnp.pad(moe_router[jx], ((0, 0), (0, 120)))
            brt = jnp.pad(moe_router_b[jx], (0, 120))[None, :]
            aux, xm = _router(x, mod_l, wr, brt)
            row_src, pos, tile_expert, n_used = _route_plan(aux)
            xg = _row_gather(xm, row_src)
            y = _moe_ffn(xg, tile_expert, n_used, moe_w_in[jx].astype(BF16),
                         moe_w_out[jx].astype(BF16))
            yg = _row_gather(y, pos).reshape(N_TOK, 2 * D_MODEL)
            x = _moe_finish(x, mod_l, aux, yg, ln_g[l, 1][None, :], ln_b[l, 1][None, :])

        pc_ = p[0:N_CTX]
        new[0].append(pc_[:, 256:512].reshape(BATCH, SEQ, NA_HEADS, HEAD_DIM))
        new[1].append(pc_[:, 512:768].reshape(BATCH, SEQ, NA_HEADS, HEAD_DIM))
        new[2].append(pc_[:, P_SWK:P_SWK + 128].reshape(BATCH, SEQ, SW_KV_HEADS, HEAD_DIM))
        new[3].append(pc_[:, P_SWV:P_SWV + 128].reshape(BATCH, SEQ, SW_KV_HEADS, HEAD_DIM))
        new[4].append(cfin[..., 0:64])
        new[5].append(cfin[..., 64])
        new[6].append(mfin[..., 0, 0])
        new[7].append(sfin)

    y_prompt = x[0:N_CTX].reshape(BATCH, SEQ, D_MODEL)
    y_sample = x[N_CTX:].reshape(DEC_BATCH, DEC_SEQ, D_MODEL)
    outs = [jnp.stack(s, axis=1) for s in new]
    return (y_prompt, y_sample) + tuple(outs)
```

```python
import functools
import math

import numpy as np
import jax
import jax.numpy as jnp
from jax import lax
from jax.experimental import pallas as pl
from jax.experimental.pallas import tpu as pltpu

F32 = jnp.float32
BF16 = jnp.bfloat16
HIGHEST = lax.Precision.HIGHEST

D_MODEL = 1024
BATCH = 16
SEQ = 256
DEPTH = 4
DEC_BATCH = 4
DEC_SEQ = 1024
PAST_LEN = 512
GRID_W = 64
HEAD_DIM = 64
NA_HEADS = 4
NA_WR = 8
NA_WC = 16
ML_HEADS = 4
DN_HEADS = 4
DN_CONV = 5
SW_HEADS = 4
SW_KV_HEADS = 2
SW_WINDOW = 128
SW_BLOCK = 128
ROPE_BASE = 10000.0
N_BRANCH = 4
FFN_DIM = 2816
N_EXPERTS = 8
ALPHA = (2 * DEPTH) ** 0.25
LN_EPS = 1e-5
NORM_EPS = 1e-6

N_CTX = BATCH * SEQ
N_LAT = DEC_BATCH * DEC_SEQ
N_TOK = N_CTX + N_LAT
BRANCH_W = NA_HEADS * HEAD_DIM

P_NA = 0
P_ML = 768
P_DN = 1536
P_MLO = 2304
P_DNG = 2560
P_SWQ = 2816
P_SWK = 3072
P_SWV = 3200
P_GATE = 3328
P_W = 3584

SEQ_BLK = 256
V7X_VMEM_LIMIT = 56 * 1024 * 1024


def _cparams(sem, vmem=V7X_VMEM_LIMIT):
    return pltpu.CompilerParams(dimension_semantics=sem, vmem_limit_bytes=vmem)


def _dot(a, b):
    return jnp.dot(a.astype(BF16), b.astype(BF16), preferred_element_type=F32)


def _dot_nt(a, b):
    return lax.dot_general(a.astype(BF16), b.astype(BF16), (((1,), (1,)), ((), ())),
                           preferred_element_type=F32)


def _dot_tn(a, b):
    return lax.dot_general(a.astype(BF16), b.astype(BF16), (((0,), (0,)), ((), ())),
                           preferred_element_type=F32)


def _dot_f32(a, b):
    return jnp.dot(a, b, precision=HIGHEST, preferred_element_type=F32)


def _sigmoid(x):
    return 1.0 / (1.0 + jnp.exp(-x))


def _softplus(x):
    return jnp.maximum(x, 0.0) + jnp.log(1.0 + jnp.exp(-jnp.abs(x)))


def _log_sigmoid(x):
    return -_softplus(-x)


def _mod_row(tok_start):
    return jnp.where(tok_start < N_CTX, 0, 1 + (tok_start - N_CTX) // DEC_SEQ)


def _layer_norm(z, g, b):
    mu = jnp.mean(z, axis=-1, keepdims=True)
    zc = z - mu
    var = jnp.mean(zc * zc, axis=-1, keepdims=True)
    return zc * lax.rsqrt(var + LN_EPS) * g + b


def _ada_kernel(c_ref, w_ref, b_ref, o_ref):
    c = c_ref[...]
    s = c * _sigmoid(c)
    o_ref[...] = _dot(s, w_ref[...]) + b_ref[...]


def _ada(cvec, ada_w, ada_b):
    tn = 1536
    n = 6 * D_MODEL
    return pl.pallas_call(
        _ada_kernel,
        out_shape=jax.ShapeDtypeStruct((DEPTH, 8, n), F32),
        grid=(DEPTH, n // tn),
        in_specs=[
            pl.BlockSpec((8, D_MODEL), lambda l, j: (0, 0)),
            pl.BlockSpec((None, D_MODEL, tn), lambda l, j: (l, 0, j)),
            pl.BlockSpec((None, 1, tn), lambda l, j: (l, 0, j)),
        ],
        out_specs=pl.BlockSpec((None, 8, tn), lambda l, j: (l, 0, j)),
        compiler_params=_cparams(("parallel", "parallel")),
        name="ada",
    )(cvec, ada_w, ada_b.reshape(DEPTH, 1, n))


def _mod_spec(chunk, tm):
    return pl.BlockSpec((None, None, 1, D_MODEL), lambda i, *_: (_mod_row(i * tm), chunk, 0, 0))


IN_TM = 512
IN_NCH = 4


def _inproj_kernel(x_ref, sh_ref, sc_ref, w_ref, wt_ref, p_ref, gt_ref):
    xm = (x_ref[...] * (1.0 + sc_ref[...]) + sh_ref[...]).astype(BF16)
    cw = P_W // IN_NCH
    for c in range(IN_NCH):
        p_ref[:, c * cw:(c + 1) * cw] = jnp.dot(xm, w_ref[:, c * cw:(c + 1) * cw],
                                                preferred_element_type=F32)
    gt = lax.dot_general(wt_ref[...], xm, (((1,), (1,)), ((), ())), preferred_element_type=F32)
    gt_ref[0] = gt[0:16]
    gt_ref[1] = gt[16:32]


def _inproj(x, mod_l, w1, wt):
    tm = IN_TM
    return pl.pallas_call(
        _inproj_kernel,
        out_shape=(jax.ShapeDtypeStruct((N_TOK, P_W), F32),
                   jax.ShapeDtypeStruct((2, 16, N_TOK), F32)),
        grid=(N_TOK // tm,),
        in_specs=[
            pl.BlockSpec((tm, D_MODEL), lambda i: (i, 0)),
            _mod_spec(0, tm),
            _mod_spec(1, tm),
            pl.BlockSpec((D_MODEL, P_W), lambda i: (0, 0)),
            pl.BlockSpec((32, D_MODEL), lambda i: (0, 0)),
        ],
        out_specs=(pl.BlockSpec((tm, P_W), lambda i: (i, 0)),
                   pl.BlockSpec((2, 16, tm), lambda i: (0, 0, i))),
        compiler_params=_cparams(("parallel",)),
        name="inproj",
    )(x, mod_l, mod_l, w1, wt)


def _ctx_attn_kernel(sink_ref, na_ref, q_ref, k_ref, v_ref, ona_ref, osw_ref):
    scale = HEAD_DIM ** -0.5
    for h in range(NA_HEADS):
        q = na_ref[:, h * 64:(h + 1) * 64]
        k = na_ref[:, 256 + h * 64:256 + (h + 1) * 64]
        v = na_ref[:, 512 + h * 64:512 + (h + 1) * 64]
        s = _dot_nt(q, k) * scale
        m = jnp.max(s, axis=1, keepdims=True)
        p = jnp.exp(s - m)
        l = jnp.sum(p, axis=1, keepdims=True)
        ona_ref[:, h * 64:(h + 1) * 64] = _dot(p, v) / l
    for h in range(SW_HEADS):
        kh = h // (SW_HEADS // SW_KV_HEADS)
        q = q_ref[:, h * 64:(h + 1) * 64]
        k = k_ref[:, kh * 64:(kh + 1) * 64]
        v = v_ref[:, kh * 64:(kh + 1) * 64]
        snk = sink_ref[h]
        s = _dot_nt(q, k) * scale
        m = jnp.maximum(jnp.max(s, axis=1, keepdims=True), snk)
        p = jnp.exp(s - m)
        l = jnp.sum(p, axis=1, keepdims=True) + jnp.exp(snk - m)
        osw_ref[:, h * 64:(h + 1) * 64] = _dot(p, v) / l


def _ctx_attn(p, sink):
    t = SEQ
    return pl.pallas_call(
        _ctx_attn_kernel,
        out_shape=(jax.ShapeDtypeStruct((N_CTX, BRANCH_W), F32),
                   jax.ShapeDtypeStruct((N_CTX, BRANCH_W), F32)),
        grid=(BATCH,),
        in_specs=[
            pl.BlockSpec(memory_space=pltpu.SMEM),
            pl.BlockSpec((t, 768), lambda b: (b, P_NA // 768)),
            pl.BlockSpec((t, 256), lambda b: (b, P_SWQ // 256)),
            pl.BlockSpec((t, 128), lambda b: (b, P_SWK // 128)),
            pl.BlockSpec((t, 128), lambda b: (b, P_SWV // 128)),
        ],
        out_specs=(pl.BlockSpec((t, BRANCH_W), lambda b: (b, 0)),
                   pl.BlockSpec((t, BRANCH_W), lambda b: (b, 0))),
        compiler_params=_cparams(("parallel",)),
        name="ctx_attn",
    )(sink, p, p, p, p)


NA_ROWS = DEC_SEQ // GRID_W
NA_KEYS = NA_WR * GRID_W


def _lat_na_kernel(q_ref, k_ref, v_ref, ck_ref, cv_ref, bias_ref, o_ref):
    scale = HEAD_DIM ** -0.5
    r = pl.program_id(1)
    rs = jnp.clip(r - NA_WR // 2, 0, NA_ROWS - NA_WR)
    start = pl.multiple_of(rs * GRID_W, GRID_W)
    kw = k_ref[pl.ds(start, NA_KEYS), :]
    vw = v_ref[pl.ds(start, NA_KEYS), :]
    for h in range(NA_HEADS):
        sl = slice(h * 64, (h + 1) * 64)
        q = q_ref[:, sl]
        s_loc = _dot_nt(q, kw[:, sl]) * scale + bias_ref[h]
        s_ctx = _dot_nt(q, ck_ref[:, sl]) * scale
        m = jnp.maximum(jnp.max(s_loc, axis=1, keepdims=True), jnp.max(s_ctx, axis=1, keepdims=True))
        p_loc = jnp.exp(s_loc - m)
        p_ctx = jnp.exp(s_ctx - m)
        l = jnp.sum(p_loc, axis=1, keepdims=True) + jnp.sum(p_ctx, axis=1, keepdims=True)
        o_ref[:, sl] = (_dot(p_loc, vw[:, sl]) + _dot(p_ctx, cv_ref[:, sl])) / l


def _lat_na(p, ck, cv, bias):
    qb0 = N_CTX // GRID_W
    kb0 = N_CTX // DEC_SEQ
    return pl.pallas_call(
        _lat_na_kernel,
        out_shape=jax.ShapeDtypeStruct((N_LAT, BRANCH_W), F32),
        grid=(DEC_BATCH, NA_ROWS),
        in_specs=[
            pl.BlockSpec((GRID_W, 256), lambda b, r: (qb0 + b * NA_ROWS + r, 0)),
            pl.BlockSpec((DEC_SEQ, 256), lambda b, r: (kb0 + b, 1)),
            pl.BlockSpec((DEC_SEQ, 256), lambda b, r: (kb0 + b, 2)),
            pl.BlockSpec((None, PAST_LEN, 256), lambda b, r: (b, 0, 0)),
            pl.BlockSpec((None, PAST_LEN, 256), lambda b, r: (b, 0, 0)),
            pl.BlockSpec((NA_HEADS, None, GRID_W, NA_KEYS), lambda b, r: (0, r, 0, 0)),
        ],
        out_specs=pl.BlockSpec((GRID_W, BRANCH_W), lambda b, r: (b * NA_ROWS + r, 0)),
        compiler_params=_cparams(("parallel", "arbitrary")),
        name="lat_na",
    )(p, p, p, ck, cv, bias)


def _na_bias_table(rpb):
    r = np.arange(NA_ROWS)
    rs = np.clip(r - NA_WR // 2, 0, NA_ROWS - NA_WR)
    dr = rs[:, None] + np.arange(NA_WR)[None, :] - r[:, None] + NA_WR - 1
    qc = np.arange(GRID_W)[:, None]
    kc = np.arange(GRID_W)[None, :]
    cs = np.clip(qc - NA_WC // 2, 0, GRID_W - NA_WC)
    ok = (kc >= cs) & (kc < cs + NA_WC)
    dc = np.clip(kc - qc + NA_WC - 1, 0, 2 * NA_WC - 2)
    oh_r = (dr.reshape(-1)[:, None] == np.arange(2 * NA_WR - 1)[None, :]).astype(np.float32)
    oh_c = (np.arange(2 * NA_WC - 1)[:, None] == dc.reshape(-1)[None, :]).astype(np.float32)
    b = jnp.einsum('pd,hdc->hpc', jnp.asarray(oh_r), rpb.astype(F32), precision=HIGHEST)
    b = jnp.einsum('hpc,cq->hpq', b, jnp.asarray(oh_c), precision=HIGHEST)
    b = b.reshape(NA_HEADS, NA_ROWS, NA_WR, GRID_W, GRID_W).transpose(0, 1, 3, 2, 4)
    b = jnp.where(jnp.asarray(ok)[None, None, :, None, :], b, -jnp.inf)
    return b.reshape(NA_HEADS, NA_ROWS, GRID_W, NA_KEYS)


SW_NB = DEC_SEQ // SW_BLOCK
SW_KEYS = 3 * SW_BLOCK


def _rope(x, cos, sin):
    w = x.shape[1]
    lane = lax.broadcasted_iota(jnp.int32, x.shape, 1)
    first = (lane % 32) < 16
    swapped = jnp.where(first, pltpu.roll(x, w - 16, 1), pltpu.roll(x, 16, 1))
    return x * cos + swapped * sin


def _lat_swa_kernel(sink_ref, q_ref, k_ref, v_ref, ck_ref, cv_ref, cos_ref, sin_ref, o_ref):
    scale = HEAD_DIM ** -0.5
    n = pl.program_id(1)
    q0 = pl.multiple_of(n * SW_BLOCK, SW_BLOCK)
    start = pl.multiple_of(jnp.clip((n - 1) * SW_BLOCK, 0, DEC_SEQ - SW_KEYS), SW_BLOCK)
    q = _rope(q_ref[...], cos_ref[pl.ds(q0, SW_BLOCK), :], sin_ref[pl.ds(q0, SW_BLOCK), :])
    kw = _rope(k_ref[pl.ds(start, SW_KEYS), :], cos_ref[pl.ds(start, SW_KEYS), 0:128],
               sin_ref[pl.ds(start, SW_KEYS), 0:128])
    vw = v_ref[pl.ds(start, SW_KEYS), :]
    qpos = q0 + lax.broadcasted_iota(jnp.int32, (SW_BLOCK, SW_KEYS), 0)
    kpos = start + lax.broadcasted_iota(jnp.int32, (SW_BLOCK, SW_KEYS), 1)
    ok = jnp.abs(qpos - kpos) <= SW_WINDOW
    for h in range(SW_HEADS):
        kh = h // (SW_HEADS // SW_KV_HEADS)
        ksl = slice(kh * 64, (kh + 1) * 64)
        qh = q[:, h * 64:(h + 1) * 64]
        snk = sink_ref[h]
        s_loc = jnp.where(ok, _dot_nt(qh, kw[:, ksl]) * scale, -jnp.inf)
        s_ctx = _dot_nt(qh, ck_ref[:, ksl]) * scale
        m = jnp.maximum(jnp.maximum(jnp.max(s_loc, axis=1, keepdims=True),
                                    jnp.max(s_ctx, axis=1, keepdims=True)), snk)
        p_loc = jnp.exp(s_loc - m)
        p_ctx = jnp.exp(s_ctx - m)
        l = (jnp.sum(p_loc, axis=1, keepdims=True) + jnp.sum(p_ctx, axis=1, keepdims=True)
             + jnp.exp(snk - m))
        o_ref[:, h * 64:(h + 1) * 64] = (_dot(p_loc, vw[:, ksl]) + _dot(p_ctx, cv_ref[:, ksl])) / l


def _lat_swa(p, sink, ck, cv, cos, sin):
    qb0 = N_CTX // SW_BLOCK
    kb0 = N_CTX // DEC_SEQ
    return pl.pallas_call(
        _lat_swa_kernel,
        out_shape=jax.ShapeDtypeStruct((N_LAT, BRANCH_W), F32),
        grid=(DEC_BATCH, SW_NB),
        in_specs=[
            pl.BlockSpec(memory_space=pltpu.SMEM),
            pl.BlockSpec((SW_BLOCK, 256), lambda b, n: (qb0 + b * SW_NB + n, P_SWQ // 256)),
            pl.BlockSpec((DEC_SEQ, 128), lambda b, n: (kb0 + b, P_SWK // 128)),
            pl.BlockSpec((DEC_SEQ, 128), lambda b, n: (kb0 + b, P_SWV // 128)),
            pl.BlockSpec((None, PAST_LEN, 128), lambda b, n: (b, 0, 0)),
            pl.BlockSpec((None, PAST_LEN, 128), lambda b, n: (b, 0, 0)),
            pl.BlockSpec((DEC_SEQ, 256), lambda b, n: (0, 0)),
            pl.BlockSpec((DEC_SEQ, 256), lambda b, n: (0, 0)),
        ],
        out_specs=pl.BlockSpec((SW_BLOCK, BRANCH_W), lambda b, n: (b * SW_NB + n, 0)),
        compiler_params=_cparams(("parallel", "arbitrary")),
        name="lat_swa",
    )(sink, p, p, p, ck, cv, cos, sin)


def _rope_tables():
    t = np.arange(DEC_SEQ)
    half = 16
    freqs = ROPE_BASE ** (-np.arange(half, dtype=np.float64) / half)
    ang_r = (t // GRID_W)[:, None] * freqs[None, :]
    ang_c = (t % GRID_W)[:, None] * freqs[None, :]
    cos = np.concatenate([np.cos(ang_r), np.cos(ang_r), np.cos(ang_c), np.cos(ang_c)], axis=1)
    sin = np.concatenate([-np.sin(ang_r), np.sin(ang_r), -np.sin(ang_c), np.sin(ang_c)], axis=1)
    return (jnp.asarray(np.tile(cos, (1, 4)), F32), jnp.asarray(np.tile(sin, (1, 4)), F32))


def _dir_masks(d, n):
    row = lax.broadcasted_iota(jnp.int32, (n, n), 0)
    col = lax.broadcasted_iota(jnp.int32, (n, n), 1)
    u = (col - row) * (1 - 2 * d)
    return row, col, u


def _seq_specs(nblk, b0_blk):
    def blk(b, d, j):
        return b0_blk + b * nblk + jnp.where(d == 0, j, nblk - 1 - j)
    return blk


def _mlstm_kernel(qkv_ref, gt_ref, gc_ref, c0_ref, m0_ref, br_ref, bc_ref,
                  h_ref, cfin_ref, mfin_ref, c_s, m_s, *, nblk):
    n = SEQ_BLK
    d = pl.program_id(1)
    j = pl.program_id(2)

    @pl.when(j == 0)
    def _():
        c_s[...] = c0_ref[...]
        m_s[...] = m0_ref[...]

    _, col, u = _dir_masks(d, n)
    incl = u <= 0
    tri_c = jnp.where(incl, 1.0, 0.0).astype(F32)
    tri_r = jnp.where(u >= 0, 1.0, 0.0).astype(F32)
    end_lane = jnp.where(d == 0, n - 1, 0)

    g_r = gt_ref[0:8, :] + br_ref[...]
    f_r = _log_sigmoid(g_r)
    bcum_r = _dot_f32(f_r, tri_r)
    g_c = gc_ref[...] + bc_ref[...]
    f_c = _log_sigmoid(g_c)
    bcum_c = _dot_f32(tri_c, f_c)

    lane_r = lax.broadcasted_iota(jnp.int32, (1, n), 1)
    ones_col = jnp.where(lax.broadcasted_iota(jnp.int32, (n, 64), 1) == 0, 1.0, 0.0).astype(F32)

    hs = range(ML_HEADS)
    q = [qkv_ref[:, h * 64:(h + 1) * 64] for h in hs]
    k = [qkv_ref[:, 256 + h * 64:256 + (h + 1) * 64] * (64 ** -0.5) for h in hs]
    vp = [jnp.concatenate([qkv_ref[:, 512 + h * 64:512 + (h + 1) * 64], ones_col], axis=1)
          for h in hs]
    b_r = [bcum_r[4 + h:5 + h, :] for h in hs]
    i_c = [g_c[:, h:h + 1] for h in hs]
    b_c = [bcum_c[:, 4 + h:5 + h] for h in hs]
    m_prev = [m_s[h][:, 0:1] for h in hs]
    cp = [c_s[h] for h in hs]
    r_row = [g_r[h:h + 1, :] - b_r[h] for h in hs]

    qk = [_dot_nt(q[h], k[h]) for h in hs]
    qc = [_dot(q[h], cp[h]) for h in hs]
    inter = [b_c[h] + m_prev[h] for h in hs]
    dmat = [jnp.where(incl, b_c[h] + r_row[h], -jnp.inf) for h in hs]
    mt = [jnp.maximum(inter[h], jnp.max(dmat[h], axis=1, keepdims=True)) for h in hs]
    s = [qk[h] * jnp.exp(dmat[h] - mt[h]) for h in hs]
    numden = [_dot(s[h], vp[h]) + jnp.exp(inter[h] - mt[h]) * qc[h] for h in hs]
    for h in hs:
        num = numden[h][:, 0:64]
        den = numden[h][:, 64:65]
        h_ref[:, h * 64:(h + 1) * 64] = num / jnp.maximum(jnp.abs(den), jnp.exp(-mt[h]))

    b_last = [jnp.sum(jnp.where(lane_r == end_lane, b_r[h], 0.0), axis=1, keepdims=True)
              for h in hs]
    m_new = [jnp.maximum(b_last[h] + m_prev[h], jnp.max(b_last[h] + r_row[h], axis=1, keepdims=True))
             for h in hs]
    kw = [k[h] * jnp.exp(b_last[h] + (i_c[h] - b_c[h]) - m_new[h]) for h in hs]
    upd = [_dot_tn(kw[h], vp[h]) for h in hs]
    for h in hs:
        c_s[h] = jnp.exp(b_last[h] + m_prev[h] - m_new[h]) * cp[h] + upd[h]
        m_s[h] = jnp.broadcast_to(m_new[h], (1, 128))

    @pl.when(j == nblk - 1)
    def _():
        cfin_ref[...] = c_s[...]
        mfin_ref[...] = m_s[...]


def _mlstm(p, gt, c0p, m0p, br, bc, nb, t, tok0):
    nblk = t // SEQ_BLK
    blk = _seq_specs(nblk, tok0 // SEQ_BLK)
    blk_out = _seq_specs(nblk, 0)
    ntok = nb * t
    return pl.pallas_call(
        functools.partial(_mlstm_kernel, nblk=nblk),
        out_shape=(jax.ShapeDtypeStruct((2, ntok, 256), F32),
                   jax.ShapeDtypeStruct((nb, 2, ML_HEADS, 64, 128), F32),
                   jax.ShapeDtypeStruct((nb, 2, ML_HEADS, 1, 128), F32)),
        grid=(nb, 2, nblk),
        in_specs=[
            pl.BlockSpec((SEQ_BLK, 768), lambda b, d, j: (blk(b, d, j), P_ML // 768)),
            pl.BlockSpec((None, 16, SEQ_BLK), lambda b, d, j: (d, 0, blk(b, d, j))),
            pl.BlockSpec((SEQ_BLK, 128), lambda b, d, j: (blk(b, d, j), P_GATE // 128 + d)),
            pl.BlockSpec((None, None, ML_HEADS, 64, 128), lambda b, d, j: (b, d, 0, 0, 0)),
            pl.BlockSpec((None, None, ML_HEADS, 1, 128), lambda b, d, j: (b, d, 0, 0, 0)),
            pl.BlockSpec((None, 8, 1), lambda b, d, j: (d, 0, 0)),
            pl.BlockSpec((None, 1, 128), lambda b, d, j: (d, 0, 0)),
        ],
        out_specs=(pl.BlockSpec((None, SEQ_BLK, 256), lambda b, d, j: (d, blk_out(b, d, j), 0)),
                   pl.BlockSpec((None, None, ML_HEADS, 64, 128), lambda b, d, j: (b, d, 0, 0, 0)),
                   pl.BlockSpec((None, None, ML_HEADS, 1, 128), lambda b, d, j: (b, d, 0, 0, 0))),
        scratch_shapes=[pltpu.VMEM((ML_HEADS, 64, 128), F32), pltpu.VMEM((ML_HEADS, 1, 128), F32)],
        compiler_params=_cparams(("parallel", "arbitrary", "arbitrary")),
        name="mlstm",
    )(p, gt, p, c0p, m0p, br, bc)


def _gdn_pre_kernel(x_ref, w_ref, o_ref, pad_s, *, t):
    pad_s[0:8, :] = jnp.zeros((8, 768), F32)
    pad_s[t + 8:t + 16, :] = jnp.zeros((8, 768), F32)
    pad_s[8:t + 8, :] = x_ref[...]
    y = jnp.zeros((t, 768), F32)
    for jj in range(DN_CONV):
        off = 8 + jj - DN_CONV // 2
        y = y + pad_s[off:off + t, :] * w_ref[jj:jj + 1, :]
    a = y * _sigmoid(y)
    for h in range(DN_HEADS):
        for part, mul in ((0, 64 ** -0.5), (256, 1.0)):
            sl = slice(part + h * 64, part + (h + 1) * 64)
            z = a[:, sl]
            o_ref[:, sl] = z * lax.rsqrt(jnp.sum(z * z, axis=1, keepdims=True) + NORM_EPS) * mul
    o_ref[:, 512:768] = a[:, 512:768]


def _gdn_pre(p, conv_w, nb, t, tok0):
    b0 = tok0 // t
    return pl.pallas_call(
        functools.partial(_gdn_pre_kernel, t=t),
        out_shape=jax.ShapeDtypeStruct((nb * t, 768), F32),
        grid=(nb,),
        in_specs=[
            pl.BlockSpec((t, 768), lambda b: (b0 + b, P_DN // 768)),
            pl.BlockSpec((8, 768), lambda b: (0, 0)),
        ],
        out_specs=pl.BlockSpec((t, 768), lambda b: (b, 0)),
        scratch_shapes=[pltpu.VMEM((t + 16, 768), F32)],
        compiler_params=_cparams(("parallel",)),
        name="gdn_pre",
    )(p, conv_w)


def _gdn_kernel(qkv_ref, gt_ref, gc_ref, s0_ref, pr_ref, pc_ref, o_ref, sfin_ref, s_s, *, nblk):
    n = SEQ_BLK
    d = pl.program_id(1)
    j = pl.program_id(2)

    @pl.when(j == 0)
    def _():
        s_s[...] = s0_ref[...]

    row, col, u = _dir_masks(d, n)
    incl = u <= 0
    strict = u < 0
    tri_c = jnp.where(incl, 1.0, 0.0).astype(F32)
    tri_r = jnp.where(u >= 0, 1.0, 0.0).astype(F32)
    end_lane = jnp.where(d == 0, n - 1, 0)
    eye = jnp.where(u == 0, 1.0, 0.0).astype(F32)

    a_r = gt_ref[8:16, :]
    g_r = -jnp.exp(pr_ref[0:8, :]) * _softplus(a_r + pr_ref[8:16, :])
    gcum_r = _dot_f32(g_r, tri_r)
    a_c = gc_ref[...]
    g_c = -jnp.exp(pc_ref[1:2, :]) * _softplus(a_c + pc_ref[0:1, :])
    gcum_c = _dot_f32(tri_c, g_c)
    beta_c = _sigmoid(a_c)
    lane_r = lax.broadcasted_iota(jnp.int32, (1, n), 1)

    hs = range(DN_HEADS)
    qn = [qkv_ref[:, h * 64:(h + 1) * 64] for h in hs]
    kn = [qkv_ref[:, 256 + h * 64:256 + (h + 1) * 64] for h in hs]
    vv = [qkv_ref[:, 512 + h * 64:512 + (h + 1) * 64] for h in hs]
    gc_row = [gcum_r[h:h + 1, :] for h in hs]
    gc_col = [gcum_c[:, 8 + h:9 + h] for h in hs]
    beta = [beta_c[:, 12 + h:13 + h] for h in hs]
    s_prev = [s_s[h] for h in hs]

    decay = [jnp.exp(jnp.where(incl, gc_col[h] - gc_row[h], -jnp.inf)) for h in hs]
    kb = [kn[h] * beta[h] for h in hs]
    gram = [_dot_nt(kb[h], kn[h]) for h in hs]
    a_mat = [jnp.where(strict, gram[h] * decay[h], 0.0) for h in hs]
    e_col = [jnp.exp(gc_col[h]) for h in hs]
    rhs = [jnp.concatenate([vv[h] * beta[h], kb[h] * e_col[h]], axis=1) for h in hs]

    same2 = (row >> 1) == (col >> 1)
    tinv = [eye - jnp.where(same2, a_mat[h], 0.0) for h in hs]
    for lb in range(1, 8):
        inner = (row >> lb) == (col >> lb)
        outer = (row >> (lb + 1)) == (col >> (lb + 1))
        off = jnp.logical_and(outer, jnp.logical_not(inner))
        m1 = [_dot(jnp.where(off, a_mat[h], 0.0), tinv[h]) for h in hs]
        tinv = [tinv[h] - _dot(tinv[h], m1[h]) for h in hs]
    sol = [_dot(tinv[h], rhs[h]) for h in hs]
    qk = [_dot_nt(qn[h], kn[h]) * decay[h] for h in hs]
    g_last = [jnp.sum(jnp.where(lane_r == end_lane, gc_row[h], 0.0), axis=1, keepdims=True)
              for h in hs]
    v_new = [sol[h][:, 0:64] - _dot(sol[h][:, 64:128], s_prev[h]) for h in hs]
    for h in hs:
        o_ref[:, h * 64:(h + 1) * 64] = _dot(qn[h] * e_col[h], s_prev[h]) + _dot(qk[h], v_new[h])
    for h in hs:
        k_dec = kn[h] * jnp.exp(g_last[h] - gc_col[h])
        s_s[h] = jnp.exp(g_last[h]) * s_prev[h] + _dot_tn(k_dec, v_new[h])

    @pl.when(j == nblk - 1)
    def _():
        sfin_ref[...] = s_s[...]


def _gdn(qkv, p, gt, s0, pr, pc, nb, t, tok0):
    nblk = t // SEQ_BLK
    blk = _seq_specs(nblk, tok0 // SEQ_BLK)
    blk0 = _seq_specs(nblk, 0)
    ntok = nb * t
    return pl.pallas_call(
        functools.partial(_gdn_kernel, nblk=nblk),
        out_shape=(jax.ShapeDtypeStruct((2, ntok, 256), F32),
                   jax.ShapeDtypeStruct((nb, 2, DN_HEADS, 64, 64), F32)),
        grid=(nb, 2, nblk),
        in_specs=[
            pl.BlockSpec((SEQ_BLK, 768), lambda b, d, j: (blk0(b, d, j), 0)),
            pl.BlockSpec((None, 16, SEQ_BLK), lambda b, d, j: (d, 0, blk(b, d, j))),
            pl.BlockSpec((SEQ_BLK, 128), lambda b, d, j: (blk(b, d, j), P_GATE // 128 + d)),
            pl.BlockSpec((None, None, DN_HEADS, 64, 64), lambda b, d, j: (b, d, 0, 0, 0)),
            pl.BlockSpec((None, 16, 1), lambda b, d, j: (d, 0, 0)),
            pl.BlockSpec((None, 2, 128), lambda b, d, j: (d, 0, 0)),
        ],
        out_specs=(pl.BlockSpec((None, SEQ_BLK, 256), lambda b, d, j: (d, blk0(b, d, j), 0)),
                   pl.BlockSpec((None, None, DN_HEADS, 64, 64), lambda b, d, j: (b, d, 0, 0, 0))),
        scratch_shapes=[pltpu.VMEM((DN_HEADS, 64, 64), F32)],
        compiler_params=_cparams(("parallel", "arbitrary", "arbitrary")),
        name="gdn",
    )(qkv, gt, p, s0, pr, pc)


MG_TM = 512


def _merge_kernel(x_ref, sh_ref, sc_ref, gt_ref, ona_ref, osw_ref, hf_ref, hb_ref, mlo_ref,
                  df_ref, db_ref, dng_ref, mlg_ref, dngain_ref, wmg_ref, wb_ref, wo_ref,
                  lng_ref, lnb_ref, o_ref):
    x = x_ref[...]
    xm = (x * (1.0 + sc_ref[...]) + sh_ref[...]).astype(BF16)
    hh = hf_ref[...] + hb_ref[...]
    dd = df_ref[...] + db_ref[...]
    ml_parts = []
    dn_parts = []
    for h in range(ML_HEADS):
        sl = slice(h * 64, (h + 1) * 64)
        z = hh[:, sl]
        mu = jnp.mean(z, axis=1, keepdims=True)
        zc = z - mu
        var = jnp.mean(zc * zc, axis=1, keepdims=True)
        ml_parts.append(zc * lax.rsqrt(var + LN_EPS))
        z = dd[:, sl]
        dn_parts.append(z * lax.rsqrt(jnp.mean(z * z, axis=1, keepdims=True) + NORM_EPS))
    o_ml = jnp.concatenate(ml_parts, axis=1) * mlg_ref[...] * _sigmoid(mlo_ref[...])
    g_pre = dng_ref[...]
    o_dn = jnp.concatenate(dn_parts, axis=1) * dngain_ref[...] * (g_pre * _sigmoid(g_pre))
    acc = jnp.zeros((x.shape[0], D_MODEL), F32)
    for nbr, o_n in enumerate((ona_ref[...], o_ml, o_dn, osw_ref[...])):
        y = _dot(o_n, wb_ref[nbr])
        gate = _sigmoid(jnp.dot(xm, wmg_ref[:, nbr * D_MODEL:(nbr + 1) * D_MODEL],
                                preferred_element_type=F32))
        acc = acc + gate * y
    mix = _dot(acc, wo_ref[...])
    o_ref[...] = _layer_norm(ALPHA * x + gt_ref[...] * mix, lng_ref[...], lnb_ref[...])


def _merge(x, mod_l, p, o_na, o_sw, hml, odn, ml_gain, dn_gain, wmg, wb, wo, ln_g, ln_b):
    tm = MG_TM
    row = lambda i: (i, 0)
    const2 = lambda i: (0, 0)
    return pl.pallas_call(
        _merge_kernel,
        out_shape=jax.ShapeDtypeStruct((N_TOK, D_MODEL), F32),
        grid=(N_TOK // tm,),
        in_specs=[
            pl.BlockSpec((tm, D_MODEL), row),
            _mod_spec(0, tm), _mod_spec(1, tm), _mod_spec(2, tm),
            pl.BlockSpec((tm, 256), row),
            pl.BlockSpec((tm, 256), row),
            pl.BlockSpec((None, tm, 256), lambda i: (0, i, 0)),
            pl.BlockSpec((None, tm, 256), lambda i: (1, i, 0)),
            pl.BlockSpec((tm, 256), lambda i: (i, P_MLO // 256)),
            pl.BlockSpec((None, tm, 256), lambda i: (0, i, 0)),
            pl.BlockSpec((None, tm, 256), lambda i: (1, i, 0)),
            pl.BlockSpec((tm, 256), lambda i: (i, P_DNG // 256)),
            pl.BlockSpec((1, 256), const2),
            pl.BlockSpec((1, 256), const2),
            pl.BlockSpec((D_MODEL, N_BRANCH * D_MODEL), const2),
            pl.BlockSpec((N_BRANCH, BRANCH_W, D_MODEL), lambda i: (0, 0, 0)),
            pl.BlockSpec((D_MODEL, D_MODEL), const2),
            pl.BlockSpec((1, D_MODEL), const2),
            pl.BlockSpec((1, D_MODEL), const2),
        ],
        out_specs=pl.BlockSpec((tm, D_MODEL), row),
        compiler_params=_cparams(("parallel",)),
        name="merge",
    )(x, mod_l, mod_l, mod_l, o_na, o_sw, hml, hml, p, odn, odn, p, ml_gain, dn_gain,
      wmg, wb, wo, ln_g, ln_b)


FF_TM = 512
FF_TH = 1408
FF_NK = FFN_DIM // FF_TH


def _ffn_kernel(x_ref, sh_ref, sc_ref, gt_ref, wg_ref, wu_ref, wo_ref, lng_ref, lnb_ref, o_ref,
                xm_s, acc_s):
    k = pl.program_id(1)

    @pl.when(k == 0)
    def _():
        xm_s[...] = (x_ref[...] * (1.0 + sc_ref[...]) + sh_ref[...]).astype(BF16)
        acc_s[...] = jnp.zeros_like(acc_s)

    xm = xm_s[...]
    hg = jnp.dot(xm, wg_ref[...], preferred_element_type=F32)
    hu = jnp.dot(xm, wu_ref[...], preferred_element_type=F32)
    a = hg * _sigmoid(hg) * hu
    acc_s[...] += _dot(a, wo_ref[...])

    @pl.when(k == FF_NK - 1)
    def _():
        z = ALPHA * x_ref[...] + gt_ref[...] * acc_s[...]
        o_ref[...] = _layer_norm(z, lng_ref[...], lnb_ref[...])


def _ffn(x, mod_l, w_in, w_out, ln_g, ln_b):
    tm = FF_TM
    return pl.pallas_call(
        _ffn_kernel,
        out_shape=jax.ShapeDtypeStruct((N_TOK, D_MODEL), F32),
        grid=(N_TOK // tm, FF_NK),
        in_specs=[
            pl.BlockSpec((tm, D_MODEL), lambda i, k: (i, 0)),
            _mod_spec(3, tm), _mod_spec(4, tm), _mod_spec(5, tm),
            pl.BlockSpec((D_MODEL, FF_TH), lambda i, k: (0, k)),
            pl.BlockSpec((D_MODEL, FF_TH), lambda i, k: (0, FF_NK + k)),
            pl.BlockSpec((FF_TH, D_MODEL), lambda i, k: (k, 0)),
            pl.BlockSpec((1, D_MODEL), lambda i, k: (0, 0)),
            pl.BlockSpec((1, D_MODEL), lambda i, k: (0, 0)),
        ],
        out_specs=pl.BlockSpec((tm, D_MODEL), lambda i, k: (i, 0)),
        scratch_shapes=[pltpu.VMEM((tm, D_MODEL), BF16), pltpu.VMEM((tm, D_MODEL), F32)],
        compiler_params=_cparams(("parallel", "arbitrary")),
        name="ffn",
    )(x, mod_l, mod_l, mod_l, w_in, w_in, w_out, ln_g, ln_b)


def _router_kernel(x_ref, sh_ref, sc_ref, wr_ref, br_ref, g_ref, xm_ref):
    xm = x_ref[...] * (1.0 + sc_ref[...]) + sh_ref[...]
    logits = _dot_f32(xm, wr_ref[...]) + br_ref[...]
    lane = lax.broadcasted_iota(jnp.int32, logits.shape, 1)
    neg = jnp.where(lane < N_EXPERTS, logits, -jnp.inf)
    v1 = jnp.max(neg, axis=1, keepdims=True)
    i1 = jnp.min(jnp.where(neg == v1, lane, 128), axis=1, keepdims=True)
    rest = jnp.where(lane == i1, -jnp.inf, neg)
    v2 = jnp.max(rest, axis=1, keepdims=True)
    i2 = jnp.min(jnp.where(rest == v2, lane, 128), axis=1, keepdims=True)
    e2 = jnp.exp(v2 - v1)
    p1 = 1.0 / (1.0 + e2)
    p2 = e2 / (1.0 + e2)
    g_ref[...] = (jnp.where(lane == 0, i1.astype(F32), 0.0) + jnp.where(lane == 1, i2.astype(F32), 0.0)
                  + jnp.where(lane == 2, p1, 0.0) + jnp.where(lane == 3, p2, 0.0))
    xm_ref[...] = xm


def _router(x, mod_l, wr, br):
    tm = 512
    return pl.pallas_call(
        _router_kernel,
        out_shape=(jax.ShapeDtypeStruct((N_TOK, 128), F32),
                   jax.ShapeDtypeStruct((N_TOK, D_MODEL), F32)),
        grid=(N_TOK // tm,),
        in_specs=[
            pl.BlockSpec((tm, D_MODEL), lambda i: (i, 0)),
            _mod_spec(3, tm), _mod_spec(4, tm),
            pl.BlockSpec((D_MODEL, 128), lambda i: (0, 0)),
            pl.BlockSpec((1, 128), lambda i: (0, 0)),
        ],
        out_specs=(pl.BlockSpec((tm, 128), lambda i: (i, 0)),
                   pl.BlockSpec((tm, D_MODEL), lambda i: (i, 0))),
        compiler_params=_cparams(("parallel",)),
        name="router",
    )(x, mod_l, mod_l, wr, br)


GATHER_ROWS = 256


def _gather_kernel(idx_ref, src_ref, o_ref, sem):
    base = pl.program_id(0) * GATHER_ROWS

    def _copy(r, src_row):
        return pltpu.make_async_copy(src_ref.at[pl.ds(src_row, 1)], o_ref.at[pl.ds(r, 1)], sem)

    def _start(r, carry):
        _copy(r, idx_ref[base + r]).start()
        return carry

    def _wait(r, carry):
        _copy(r, 0).wait()
        return carry

    lax.fori_loop(0, GATHER_ROWS, _start, 0, unroll=8)
    lax.fori_loop(0, GATHER_ROWS, _wait, 0, unroll=8)


def _row_gather(src, idx):
    m = idx.shape[0]
    return pl.pallas_call(
        _gather_kernel,
        out_shape=jax.ShapeDtypeStruct((m, src.shape[1]), src.dtype),
        grid_spec=pltpu.PrefetchScalarGridSpec(
            num_scalar_prefetch=1,
            grid=(m // GATHER_ROWS,),
            in_specs=[pl.BlockSpec(memory_space=pl.ANY)],
            out_specs=pl.BlockSpec((GATHER_ROWS, src.shape[1]), lambda i, idx_ref: (i, 0)),
            scratch_shapes=[pltpu.SemaphoreType.DMA(())],
        ),
        compiler_params=_cparams(("arbitrary",)),
        name="row_gather",
    )(idx, src)


MOE_TM = 512
MOE_NT = 2 * N_TOK // MOE_TM + N_EXPERTS


def _route_plan(aux):
    e = jnp.concatenate([aux[:, 0], aux[:, 1]]).astype(jnp.int32)
    tok = jnp.tile(jnp.arange(N_TOK, dtype=jnp.int32), 2)
    onehot = (e[:, None] == jnp.arange(N_EXPERTS, dtype=jnp.int32)[None, :]).astype(jnp.int32)
    csum = jnp.cumsum(onehot, axis=0)
    rank = jnp.sum((csum - onehot) * onehot, axis=1)
    counts = csum[-1]
    tiles_per = (counts + MOE_TM - 1) // MOE_TM
    tile_end = jnp.cumsum(tiles_per)
    tile_start = tile_end - tiles_per
    dest = jnp.sum(onehot * tile_start[None, :], axis=1) * MOE_TM + rank
    row_src = jnp.zeros((MOE_NT * MOE_TM,), jnp.int32).at[dest].set(tok)
    n_used = tile_end[-1]
    tiles = jnp.minimum(jnp.arange(MOE_NT, dtype=jnp.int32), n_used - 1)
    tile_expert = jnp.sum((tiles[:, None] >= tile_end[None, :]).astype(jnp.int32), axis=1)
    pos = jnp.stack([dest[:N_TOK], dest[N_TOK:]], axis=1).reshape(2 * N_TOK)
    return row_src, pos, tile_expert.astype(jnp.int32), n_used.reshape(1).astype(jnp.int32)


def _moe_ffn_kernel(te_ref, nu_ref, x_ref, wg_ref, wu_ref, wo_ref, o_ref, xm_s, acc_s):
    i = pl.program_id(0)
    k = pl.program_id(1)
    used = i < nu_ref[0]

    @pl.when(jnp.logical_and(used, k == 0))
    def _():
        xm_s[...] = x_ref[...].astype(BF16)
        acc_s[...] = jnp.zeros_like(acc_s)

    @pl.when(used)
    def _():
        xm = xm_s[...]
        hg = jnp.dot(xm, wg_ref[...], preferred_element_type=F32)
        hu = jnp.dot(xm, wu_ref[...], preferred_element_type=F32)
        a = hg * _sigmoid(hg) * hu
        acc_s[...] += _dot(a, wo_ref[...])

    @pl.when(k == FF_NK - 1)
    def _():
        o_ref[...] = jnp.where(used, acc_s[...], 0.0)


def _moe_ffn(xg, tile_expert, n_used, w_in, w_out):
    tm = MOE_TM

    def kk(i, k, nu):
        return jnp.where(i < nu[0], k, FF_NK - 1)

    return pl.pallas_call(
        _moe_ffn_kernel,
        out_shape=jax.ShapeDtypeStruct((MOE_NT * tm, D_MODEL), F32),
        grid_spec=pltpu.PrefetchScalarGridSpec(
            num_scalar_prefetch=2,
            grid=(MOE_NT, FF_NK),
            in_specs=[
                pl.BlockSpec((tm, D_MODEL), lambda i, k, te, nu: (i, 0)),
                pl.BlockSpec((None, D_MODEL, FF_TH), lambda i, k, te, nu: (te[i], 0, kk(i, k, nu))),
                pl.BlockSpec((None, D_MODEL, FF_TH),
                             lambda i, k, te, nu: (te[i], 0, FF_NK + kk(i, k, nu))),
                pl.BlockSpec((None, FF_TH, D_MODEL), lambda i, k, te, nu: (te[i], kk(i, k, nu), 0)),
            ],
            out_specs=pl.BlockSpec((tm, D_MODEL), lambda i, k, te, nu: (i, 0)),
            scratch_shapes=[pltpu.VMEM((tm, D_MODEL), BF16), pltpu.VMEM((tm, D_MODEL), F32)],
        ),
        compiler_params=_cparams(("arbitrary", "arbitrary")),
        name="moe_ffn",
    )(tile_expert, n_used, xg, w_in, w_in, w_out)


def _moe_finish_kernel(x_ref, gt_ref, aux_ref, y_ref, lng_ref, lnb_ref, o_ref):
    aux = aux_ref[...]
    ff = aux[:, 2:3] * y_ref[:, 0:D_MODEL] + aux[:, 3:4] * y_ref[:, D_MODEL:2 * D_MODEL]
    z = ALPHA * x_ref[...] + gt_ref[...] * ff
    o_ref[...] = _layer_norm(z, lng_ref[...], lnb_ref[...])


def _moe_finish(x, mod_l, aux, yg, ln_g, ln_b):
    tm = 512
    return pl.pallas_call(
        _moe_finish_kernel,
        out_shape=jax.ShapeDtypeStruct((N_TOK, D_MODEL), F32),
        grid=(N_TOK // tm,),
        in_specs=[
            pl.BlockSpec((tm, D_MODEL), lambda i: (i, 0)),
            _mod_spec(5, tm),
            pl.BlockSpec((tm, 128), lambda i: (i, 0)),
            pl.BlockSpec((tm, 2 * D_MODEL), lambda i: (i, 0)),
            pl.BlockSpec((1, D_MODEL), lambda i: (0, 0)),
            pl.BlockSpec((1, D_MODEL), lambda i: (0, 0)),
        ],
        out_specs=pl.BlockSpec((tm, D_MODEL), lambda i: (i, 0)),
        compiler_params=_cparams(("parallel",)),
        name="moe_finish",
    )(x, mod_l, aux, yg, ln_g, ln_b)


def _split_w_in(w):
    na = w[:, 0:768]
    ml = w[:, 768:1536]
    mlg = w[:, 1536:1552]
    mlo = w[:, 1552:1808]
    dn = w[:, 1808:2576]
    dna = w[:, 2576:2584]
    dnb = w[:, 2584:2592]
    dng = w[:, 2592:2848]
    sw = w[:, 2848:3360]
    mg = w[:, 3360:7456]
    zpad = jnp.zeros((D_MODEL, 112), w.dtype)
    gates = []
    for d in range(2):
        gates.append(jnp.concatenate([mlg[:, 4 * d:4 * d + 4], mlg[:, 8 + 4 * d:12 + 4 * d],
                                      dna[:, 4 * d:4 * d + 4], dnb[:, 4 * d:4 * d + 4]], axis=1))
    w1 = jnp.concatenate([na, ml, dn, mlo, dng, sw, gates[0], zpad, gates[1], zpad], axis=1)
    wt = jnp.concatenate(gates, axis=1).T
    return w1.astype(BF16), wt.astype(BF16), mg.astype(BF16)


def kernel(x_prompt, x_sample, c, cache_na_k, cache_na_v, cache_sw_k, cache_sw_v, state_ml_c, state_ml_n, state_ml_m, state_dn_s, c_ctx, ada_w, ada_b, w_in, na_rpb, ml_gate_b, ml_norm_g, dn_conv_w, dn_a_log, dn_dt_bias, dn_norm_g, sw_sink, w_branch, w_out, ln_g, ln_b, ffn_w_in, ffn_w_out, moe_router, moe_router_b, moe_w_in, moe_w_out):
    x = jnp.concatenate([x_prompt.reshape(N_CTX, D_MODEL), x_sample.reshape(N_LAT, D_MODEL)], axis=0)
    cvec = jnp.concatenate([c_ctx[None, :], c, jnp.zeros((3, D_MODEL), F32)], axis=0)
    mod = _ada(cvec, ada_w, ada_b).reshape(DEPTH, 8, 6, 1, D_MODEL)
    cos, sin = _rope_tables()

    zc_ctx = jnp.zeros((BATCH, 2, ML_HEADS, 64, 128), F32)
    zm_ctx = jnp.zeros((BATCH, 2, ML_HEADS, 1, 128), F32)
    zs_ctx = jnp.zeros((BATCH, 2, DN_HEADS, 64, 64), F32)
    pad63 = jnp.zeros((DEC_BATCH, 2, ML_HEADS, 64, 63), F32)

    new = [[] for _ in range(8)]
    for l in range(DEPTH):
        w1, wt, wmg = _split_w_in(w_in[l])
        mod_l = mod[l]
        p, gt = _inproj(x, mod_l, w1, wt)

        o_na_c, o_sw_c = _ctx_attn(p, sw_sink[l])
        ck = cache_na_k[:, l].reshape(DEC_BATCH, PAST_LEN, 256)
        cv = cache_na_v[:, l].reshape(DEC_BATCH, PAST_LEN, 256)
        o_na_l = _lat_na(p, ck, cv, _na_bias_table(na_rpb[l]))
        sk = cache_sw_k[:, l].reshape(DEC_BATCH, PAST_LEN, 128)
        sv = cache_sw_v[:, l].reshape(DEC_BATCH, PAST_LEN, 128)
        o_sw_l = _lat_swa(p, sw_sink[l], sk, sv, cos, sin)

        gb = ml_gate_b[l]
        br = jnp.stack([jnp.concatenate([gb[0], gb[2]]), jnp.concatenate([gb[1], gb[3]])])[:, :, None]
        bc = jnp.pad(br[:, :, 0], ((0, 0), (0, 120)))[:, None, :]
        h_c, cfin, mfin = _mlstm(p, gt, zc_ctx, zm_ctx, br, bc, BATCH, SEQ, 0)
        c0p = jnp.concatenate([state_ml_c[:, l], state_ml_n[:, l][..., None], pad63], axis=-1)
        m0p = jnp.broadcast_to(state_ml_m[:, l][..., None, None], (DEC_BATCH, 2, ML_HEADS, 1, 128))
        h_l, _, _ = _mlstm(p, gt, c0p, m0p, br, bc, DEC_BATCH, DEC_SEQ, N_CTX)

        conv_w = jnp.pad(dn_conv_w[l], ((0, 3), (0, 0)))
        z4 = jnp.zeros((2, 4), F32)
        pr = jnp.concatenate([dn_a_log[l], z4, dn_dt_bias[l], z4], axis=1)[:, :, None]
        pc = jnp.stack([jnp.pad(dn_dt_bias[l], ((0, 0), (8, 116))),
                        jnp.pad(dn_a_log[l], ((0, 0), (8, 116)))], axis=1)
        qkv_c = _gdn_pre(p, conv_w, BATCH, SEQ, 0)
        o_c, sfin = _gdn(qkv_c, p, gt, zs_ctx, pr, pc, BATCH, SEQ, 0)
        qkv_l = _gdn_pre(p, conv_w, DEC_BATCH, DEC_SEQ, N_CTX)
        o_l, _ = _gdn(qkv_l, p, gt, state_dn_s[:, l], pr, pc, DEC_BATCH, DEC_SEQ, N_CTX)

        o_na = jnp.concatenate([o_na_c, o_na_l], axis=0)
        o_sw = jnp.concatenate([o_sw_c, o_sw_l], axis=0)
        hml = jnp.concatenate([h_c, h_l], axis=1)
        odn = jnp.concatenate([o_c, o_l], axis=1)
        x = _merge(x, mod_l, p, o_na, o_sw, hml, odn, ml_norm_g[l][None, :],
                   jnp.tile(dn_norm_g[l], DN_HEADS)[None, :], wmg, w_branch[l].astype(BF16),
                   w_out[l].astype(BF16), ln_g[l, 0][None, :], ln_b[l, 0][None, :])

        jx = l // 2
        if l % 2 == 0:
            x = _ffn(x, mod_l, ffn_w_in[jx].astype(BF16), ffn_w_out[jx].astype(BF16),
                     ln_g[l, 1][None, :], ln_b[l, 1][None, :])
        else:
            wr = jnp.pad(moe_router[jx], ((0, 0), (0, 120)))
            brt = jnp.pad(moe_router_b[jx], (0, 120))[None, :]
            aux, xm = _router(x, mod_l, wr, brt)
            row_src, pos, tile_expert, n_used = _route_plan(aux)
            xg = _row_gather(xm, row_src)
            y = _moe_ffn(xg, tile_expert, n_used, moe_w_in[jx].astype(BF16),
                         moe_w_out[jx].astype(BF16))
            yg = _row_gather(y, pos).reshape(N_TOK, 2 * D_MODEL)
            x = _moe_finish(x, mod_l, aux, yg, ln_g[l, 1][None, :], ln_b[l, 1][None, :])

        pc_ = p[0:N_CTX]
        new[0].append(pc_[:, 256:512].reshape(BATCH, SEQ, NA_HEADS, HEAD_DIM))
        new[1].append(pc_[:, 512:768].reshape(BATCH, SEQ, NA_HEADS, HEAD_DIM))
        new[2].append(pc_[:, P_SWK:P_SWK + 128].reshape(BATCH, SEQ, SW_KV_HEADS, HEAD_DIM))
        new[3].append(pc_[:, P_SWV:P_SWV + 128].reshape(BATCH, SEQ, SW_KV_HEADS, HEAD_DIM))
        new[4].append(cfin[..., 0:64])
        new[5].append(cfin[..., 64])
        new[6].append(mfin[..., 0, 0])
        new[7].append(sfin)

    y_prompt = x[0:N_CTX].reshape(BATCH, SEQ, D_MODEL)
    y_sample = x[N_CTX:].reshape(DEC_BATCH, DEC_SEQ, D_MODEL)
    outs = [jnp.stack(s, axis=1) for s in new]
    return (y_prompt, y_sample) + tuple(outs)
```

```python
import functools
import math

import numpy as np
import jax
import jax.numpy as jnp
from jax import lax
from jax.experimental import pallas as pl
from jax.experimental.pallas import tpu as pltpu

F32 = jnp.float32
BF16 = jnp.bfloat16
HIGHEST = lax.Precision.HIGHEST

D_MODEL = 1024
BATCH = 16
SEQ = 256
DEPTH = 4
DEC_BATCH = 4
DEC_SEQ = 1024
PAST_LEN = 512
GRID_W = 64
HEAD_DIM = 64
NA_HEADS = 4
NA_WR = 8
NA_WC = 16
ML_HEADS = 4
DN_HEADS = 4
DN_CONV = 5
SW_HEADS = 4
SW_KV_HEADS = 2
SW_WINDOW = 128
SW_BLOCK = 128
ROPE_BASE = 10000.0
N_BRANCH = 4
FFN_DIM = 2816
N_EXPERTS = 8
ALPHA = (2 * DEPTH) ** 0.25
LN_EPS = 1e-5
NORM_EPS = 1e-6

N_CTX = BATCH * SEQ
N_LAT = DEC_BATCH * DEC_SEQ
N_TOK = N_CTX + N_LAT
BRANCH_W = NA_HEADS * HEAD_DIM

P_NA = 0
P_ML = 768
P_DN = 1536
P_MLO = 2304
P_DNG = 2560
P_SWQ = 2816
P_SWK = 3072
P_SWV = 3200
P_GATE = 3328
P_W = 3584

SEQ_BLK = 256
V7X_VMEM_LIMIT = 56 * 1024 * 1024


def _cparams(sem, vmem=V7X_VMEM_LIMIT):
    return pltpu.CompilerParams(dimension_semantics=sem, vmem_limit_bytes=vmem)


def _dot(a, b):
    return jnp.dot(a.astype(BF16), b.astype(BF16), preferred_element_type=F32)


def _dot_nt(a, b):
    return lax.dot_general(a.astype(BF16), b.astype(BF16), (((1,), (1,)), ((), ())),
                           preferred_element_type=F32)


def _dot_tn(a, b):
    return lax.dot_general(a.astype(BF16), b.astype(BF16), (((0,), (0,)), ((), ())),
                           preferred_element_type=F32)


def _dot_f32(a, b):
    return jnp.dot(a, b, precision=HIGHEST, preferred_element_type=F32)


def _sigmoid(x):
    return 1.0 / (1.0 + jnp.exp(-x))


def _softplus(x):
    return jnp.maximum(x, 0.0) + jnp.log(1.0 + jnp.exp(-jnp.abs(x)))


def _log_sigmoid(x):
    return -_softplus(-x)


def _mod_row(tok_start):
    return jnp.where(tok_start < N_CTX, 0, 1 + (tok_start - N_CTX) // DEC_SEQ)


def _layer_norm(z, g, b):
    mu = jnp.mean(z, axis=-1, keepdims=True)
    zc = z - mu
    var = jnp.mean(zc * zc, axis=-1, keepdims=True)
    return zc * lax.rsqrt(var + LN_EPS) * g + b


def _ada_kernel(c_ref, w_ref, b_ref, o_ref):
    c = c_ref[...]
    s = c * _sigmoid(c)
    o_ref[...] = _dot(s, w_ref[...]) + b_ref[...]


def _ada(cvec, ada_w, ada_b):
    tn = 1536
    n = 6 * D_MODEL
    return pl.pallas_call(
        _ada_kernel,
        out_shape=jax.ShapeDtypeStruct((DEPTH, 8, n), F32),
        grid=(DEPTH, n // tn),
        in_specs=[
            pl.BlockSpec((8, D_MODEL), lambda l, j: (0, 0)),
            pl.BlockSpec((None, D_MODEL, tn), lambda l, j: (l, 0, j)),
            pl.BlockSpec((None, 1, tn), lambda l, j: (l, 0, j)),
        ],
        out_specs=pl.BlockSpec((None, 8, tn), lambda l, j: (l, 0, j)),
        compiler_params=_cparams(("parallel", "parallel")),
        name="ada",
    )(cvec, ada_w, ada_b.reshape(DEPTH, 1, n))


def _mod_spec(chunk, tm):
    return pl.BlockSpec((None, None, 1, D_MODEL), lambda i, *_: (_mod_row(i * tm), chunk, 0, 0))


IN_TM = 512
IN_NCH = 4


def _inproj_kernel(x_ref, sh_ref, sc_ref, w_ref, wt_ref, p_ref, gt_ref):
    xm = (x_ref[...] * (1.0 + sc_ref[...]) + sh_ref[...]).astype(BF16)
    cw = P_W // IN_NCH
    for c in range(IN_NCH):
        p_ref[:, c * cw:(c + 1) * cw] = jnp.dot(xm, w_ref[:, c * cw:(c + 1) * cw],
                                                preferred_element_type=F32)
    gt = lax.dot_general(wt_ref[...], xm, (((1,), (1,)), ((), ())), preferred_element_type=F32)
    gt_ref[0] = gt[0:16]
    gt_ref[1] = gt[16:32]


def _inproj(x, mod_l, w1, wt):
    tm = IN_TM
    return pl.pallas_call(
        _inproj_kernel,
        out_shape=(jax.ShapeDtypeStruct((N_TOK, P_W), F32),
                   jax.ShapeDtypeStruct((2, 16, N_TOK), F32)),
        grid=(N_TOK // tm,),
        in_specs=[
            pl.BlockSpec((tm, D_MODEL), lambda i: (i, 0)),
            _mod_spec(0, tm),
            _mod_spec(1, tm),
            pl.BlockSpec((D_MODEL, P_W), lambda i: (0, 0)),
            pl.BlockSpec((32, D_MODEL), lambda i: (0, 0)),
        ],
        out_specs=(pl.BlockSpec((tm, P_W), lambda i: (i, 0)),
                   pl.BlockSpec((2, 16, tm), lambda i: (0, 0, i))),
        compiler_params=_cparams(("parallel",)),
        name="inproj",
    )(x, mod_l, mod_l, w1, wt)


def _ctx_attn_kernel(sink_ref, na_ref, q_ref, k_ref, v_ref, ona_ref, osw_ref):
    scale = HEAD_DIM ** -0.5
    for h in range(NA_HEADS):
        q = na_ref[:, h * 64:(h + 1) * 64]
        k = na_ref[:, 256 + h * 64:256 + (h + 1) * 64]
        v = na_ref[:, 512 + h * 64:512 + (h + 1) * 64]
        s = _dot_nt(q, k) * scale
        m = jnp.max(s, axis=1, keepdims=True)
        p = jnp.exp(s - m)
        l = jnp.sum(p, axis=1, keepdims=True)
        ona_ref[:, h * 64:(h + 1) * 64] = _dot(p, v) / l
    for h in range(SW_HEADS):
        kh = h // (SW_HEADS // SW_KV_HEADS)
        q = q_ref[:, h * 64:(h + 1) * 64]
        k = k_ref[:, kh * 64:(kh + 1) * 64]
        v = v_ref[:, kh * 64:(kh + 1) * 64]
        snk = sink_ref[h]
        s = _dot_nt(q, k) * scale
        m = jnp.maximum(jnp.max(s, axis=1, keepdims=True), snk)
        p = jnp.exp(s - m)
        l = jnp.sum(p, axis=1, keepdims=True) + jnp.exp(snk - m)
        osw_ref[:, h * 64:(h + 1) * 64] = _dot(p, v) / l


def _ctx_attn(p, sink):
    t = SEQ
    return pl.pallas_call(
        _ctx_attn_kernel,
        out_shape=(jax.ShapeDtypeStruct((N_CTX, BRANCH_W), F32),
                   jax.ShapeDtypeStruct((N_CTX, BRANCH_W), F32)),
        grid=(BATCH,),
        in_specs=[
            pl.BlockSpec(memory_space=pltpu.SMEM),
            pl.BlockSpec((t, 768), lambda b: (b, P_NA // 768)),
            pl.BlockSpec((t, 256), lambda b: (b, P_SWQ // 256)),
            pl.BlockSpec((t, 128), lambda b: (b, P_SWK // 128)),
            pl.BlockSpec((t, 128), lambda b: (b, P_SWV // 128)),
        ],
        out_specs=(pl.BlockSpec((t, BRANCH_W), lambda b: (b, 0)),
                   pl.BlockSpec((t, BRANCH_W), lambda b: (b, 0))),
        compiler_params=_cparams(("parallel",)),
        name="ctx_attn",
    )(sink, p, p, p, p)


NA_ROWS = DEC_SEQ // GRID_W
NA_KEYS = NA_WR * GRID_W


def _lat_na_kernel(q_ref, k_ref, v_ref, ck_ref, cv_ref, bias_ref, o_ref):
    scale = HEAD_DIM ** -0.5
    r = pl.program_id(1)
    rs = jnp.clip(r - NA_WR // 2, 0, NA_ROWS - NA_WR)
    start = pl.multiple_of(rs * GRID_W, GRID_W)
    kw = k_ref[pl.ds(start, NA_KEYS), :]
    vw = v_ref[pl.ds(start, NA_KEYS), :]
    for h in range(NA_HEADS):
        sl = slice(h * 64, (h + 1) * 64)
        q = q_ref[:, sl]
        s_loc = _dot_nt(q, kw[:, sl]) * scale + bias_ref[h]
        s_ctx = _dot_nt(q, ck_ref[:, sl]) * scale
        m = jnp.maximum(jnp.max(s_loc, axis=1, keepdims=True), jnp.max(s_ctx, axis=1, keepdims=True))
        p_loc = jnp.exp(s_loc - m)
        p_ctx = jnp.exp(s_ctx - m)
        l = jnp.sum(p_loc, axis=1, keepdims=True) + jnp.sum(p_ctx, axis=1, keepdims=True)
        o_ref[:, sl] = (_dot(p_loc, vw[:, sl]) + _dot(p_ctx, cv_ref[:, sl])) / l


def _lat_na(p, ck, cv, bias):
    qb0 = N_CTX // GRID_W
    kb0 = N_CTX // DEC_SEQ
    return pl.pallas_call(
        _lat_na_kernel,
        out_shape=jax.ShapeDtypeStruct((N_LAT, BRANCH_W), F32),
        grid=(DEC_BATCH, NA_ROWS),
        in_specs=[
            pl.BlockSpec((GRID_W, 256), lambda b, r: (qb0 + b * NA_ROWS + r, 0)),
            pl.BlockSpec((DEC_SEQ, 256), lambda b, r: (kb0 + b, 1)),
            pl.BlockSpec((DEC_SEQ, 256), lambda b, r: (kb0 + b, 2)),
            pl.BlockSpec((None, PAST_LEN, 256), lambda b, r: (b, 0, 0)),
            pl.BlockSpec((None, PAST_LEN, 256), lambda b, r: (b, 0, 0)),
            pl.BlockSpec((NA_HEADS, None, GRID_W, NA_KEYS), lambda b, r: (0, r, 0, 0)),
        ],
        out_specs=pl.BlockSpec((GRID_W, BRANCH_W), lambda b, r: (b * NA_ROWS + r, 0)),
        compiler_params=_cparams(("parallel", "arbitrary")),
        name="lat_na",
    )(p, p, p, ck, cv, bias)


def _na_bias_table(rpb):
    r = np.arange(NA_ROWS)
    rs = np.clip(r - NA_WR // 2, 0, NA_ROWS - NA_WR)
    dr = rs[:, None] + np.arange(NA_WR)[None, :] - r[:, None] + NA_WR - 1
    qc = np.arange(GRID_W)[:, None]
    kc = np.arange(GRID_W)[None, :]
    cs = np.clip(qc - NA_WC // 2, 0, GRID_W - NA_WC)
    ok = (kc >= cs) & (kc < cs + NA_WC)
    dc = np.clip(kc - qc + NA_WC - 1, 0, 2 * NA_WC - 2)
    oh_r = (dr.reshape(-1)[:, None] == np.arange(2 * NA_WR - 1)[None, :]).astype(np.float32)
    oh_c = (np.arange(2 * NA_WC - 1)[:, None] == dc.reshape(-1)[None, :]).astype(np.float32)
    b = jnp.einsum('pd,hdc->hpc', jnp.asarray(oh_r), rpb.astype(F32), precision=HIGHEST)
    b = jnp.einsum('hpc,cq->hpq', b, jnp.asarray(oh_c), precision=HIGHEST)
    b = b.reshape(NA_HEADS, NA_ROWS, NA_WR, GRID_W, GRID_W).transpose(0, 1, 3, 2, 4)
    b = jnp.where(jnp.asarray(ok)[None, None, :, None, :], b, -jnp.inf)
    return b.reshape(NA_HEADS, NA_ROWS, GRID_W, NA_KEYS)


SW_NB = DEC_SEQ // SW_BLOCK
SW_KEYS = 3 * SW_BLOCK


def _rope(x, cos, sin):
    w = x.shape[1]
    lane = lax.broadcasted_iota(jnp.int32, x.shape, 1)
    first = (lane % 32) < 16
    swapped = jnp.where(first, pltpu.roll(x, w - 16, 1), pltpu.roll(x, 16, 1))
    return x * cos + swapped * sin


def _lat_swa_kernel(sink_ref, q_ref, k_ref, v_ref, ck_ref, cv_ref, cos_ref, sin_ref, o_ref):
    scale = HEAD_DIM ** -0.5
    n = pl.program_id(1)
    q0 = pl.multiple_of(n * SW_BLOCK, SW_BLOCK)
    start = pl.multiple_of(jnp.clip((n - 1) * SW_BLOCK, 0, DEC_SEQ - SW_KEYS), SW_BLOCK)
    q = _rope(q_ref[...], cos_ref[pl.ds(q0, SW_BLOCK), :], sin_ref[pl.ds(q0, SW_BLOCK), :])
    kw = _rope(k_ref[pl.ds(start, SW_KEYS), :], cos_ref[pl.ds(start, SW_KEYS), 0:128],
               sin_ref[pl.ds(start, SW_KEYS), 0:128])
    vw = v_ref[pl.ds(start, SW_KEYS), :]
    qpos = q0 + lax.broadcasted_iota(jnp.int32, (SW_BLOCK, SW_KEYS), 0)
    kpos = start + lax.broadcasted_iota(jnp.int32, (SW_BLOCK, SW_KEYS), 1)
    ok = jnp.abs(qpos - kpos) <= SW_WINDOW
    for h in range(SW_HEADS):
        kh = h // (SW_HEADS // SW_KV_HEADS)
        ksl = slice(kh * 64, (kh + 1) * 64)
        qh = q[:, h * 64:(h + 1) * 64]
        snk = sink_ref[h]
        s_loc = jnp.where(ok, _dot_nt(qh, kw[:, ksl]) * scale, -jnp.inf)
        s_ctx = _dot_nt(qh, ck_ref[:, ksl]) * scale
        m = jnp.maximum(jnp.maximum(jnp.max(s_loc, axis=1, keepdims=True),
                                    jnp.max(s_ctx, axis=1, keepdims=True)), snk)
        p_loc = jnp.exp(s_loc - m)
        p_ctx = jnp.exp(s_ctx - m)
        l = (jnp.sum(p_loc, axis=1, keepdims=True) + jnp.sum(p_ctx, axis=1, keepdims=True)
             + jnp.exp(snk - m))
        o_ref[:, h * 64:(h + 1) * 64] = (_dot(p_loc, vw[:, ksl]) + _dot(p_ctx, cv_ref[:, ksl])) / l


def _lat_swa(p, sink, ck, cv, cos, sin):
    qb0 = N_CTX // SW_BLOCK
    kb0 = N_CTX // DEC_SEQ
    return pl.pallas_call(
        _lat_swa_kernel,
        out_shape=jax.ShapeDtypeStruct((N_LAT, BRANCH_W), F32),
        grid=(DEC_BATCH, SW_NB),
        in_specs=[
            pl.BlockSpec(memory_space=pltpu.SMEM),
            pl.BlockSpec((SW_BLOCK, 256), lambda b, n: (qb0 + b * SW_NB + n, P_SWQ // 256)),
            pl.BlockSpec((DEC_SEQ, 128), lambda b, n: (kb0 + b, P_SWK // 128)),
            pl.BlockSpec((DEC_SEQ, 128), lambda b, n: (kb0 + b, P_SWV // 128)),
            pl.BlockSpec((None, PAST_LEN, 128), lambda b, n: (b, 0, 0)),
            pl.BlockSpec((None, PAST_LEN, 128), lambda b, n: (b, 0, 0)),
            pl.BlockSpec((DEC_SEQ, 256), lambda b, n: (0, 0)),
            pl.BlockSpec((DEC_SEQ, 256), lambda b, n: (0, 0)),
        ],
        out_specs=pl.BlockSpec((SW_BLOCK, BRANCH_W), lambda b, n: (b * SW_NB + n, 0)),
        compiler_params=_cparams(("parallel", "arbitrary")),
        name="lat_swa",
    )(sink, p, p, p, ck, cv, cos, sin)


def _rope_tables():
    t = np.arange(DEC_SEQ)
    half = 16
    freqs = ROPE_BASE ** (-np.arange(half, dtype=np.float64) / half)
    ang_r = (t // GRID_W)[:, None] * freqs[None, :]
    ang_c = (t % GRID_W)[:, None] * freqs[None, :]
    cos = np.concatenate([np.cos(ang_r), np.cos(ang_r), np.cos(ang_c), np.cos(ang_c)], axis=1)
    sin = np.concatenate([-np.sin(ang_r), np.sin(ang_r), -np.sin(ang_c), np.sin(ang_c)], axis=1)
    return (jnp.asarray(np.tile(cos, (1, 4)), F32), jnp.asarray(np.tile(sin, (1, 4)), F32))


def _dir_masks(d, n):
    row = lax.broadcasted_iota(jnp.int32, (n, n), 0)
    col = lax.broadcasted_iota(jnp.int32, (n, n), 1)
    u = (col - row) * (1 - 2 * d)
    return row, col, u


def _seq_specs(nblk, b0_blk):
    def blk(b, d, j):
        return b0_blk + b * nblk + jnp.where(d == 0, j, nblk - 1 - j)
    return blk


def _mlstm_kernel(qkv_ref, gt_ref, gc_ref, c0_ref, m0_ref, br_ref, bc_ref,
                  h_ref, cfin_ref, mfin_ref, c_s, m_s, *, nblk):
    n = SEQ_BLK
    d = pl.program_id(1)
    j = pl.program_id(2)

    @pl.when(j == 0)
    def _():
        c_s[...] = c0_ref[...]
        m_s[...] = m0_ref[...]

    _, col, u = _dir_masks(d, n)
    incl = u <= 0
    tri_c = jnp.where(incl, 1.0, 0.0).astype(F32)
    tri_r = jnp.where(u >= 0, 1.0, 0.0).astype(F32)
    end_lane = jnp.where(d == 0, n - 1, 0)

    g_r = gt_ref[0:8, :] + br_ref[...]
    f_r = _log_sigmoid(g_r)
    bcum_r = _dot_f32(f_r, tri_r)
    g_c = gc_ref[...] + bc_ref[...]
    f_c = _log_sigmoid(g_c)
    bcum_c = _dot_f32(tri_c, f_c)

    lane_r = lax.broadcasted_iota(jnp.int32, (1, n), 1)
    ones_col = jnp.where(lax.broadcasted_iota(jnp.int32, (n, 64), 1) == 0, 1.0, 0.0).astype(F32)

    hs = range(ML_HEADS)
    q = [qkv_ref[:, h * 64:(h + 1) * 64] for h in hs]
    k = [qkv_ref[:, 256 + h * 64:256 + (h + 1) * 64] * (64 ** -0.5) for h in hs]
    vp = [jnp.concatenate([qkv_ref[:, 512 + h * 64:512 + (h + 1) * 64], ones_col], axis=1)
          for h in hs]
    b_r = [bcum_r[4 + h:5 + h, :] for h in hs]
    i_c = [g_c[:, h:h + 1] for h in hs]
    b_c = [bcum_c[:, 4 + h:5 + h] for h in hs]
    m_prev = [m_s[h][:, 0:1] for h in hs]
    cp = [c_s[h] for h in hs]
    r_row = [g_r[h:h + 1, :] - b_r[h] for h in hs]

    qk = [_dot_nt(q[h], k[h]) for h in hs]
    qc = [_dot(q[h], cp[h]) for h in hs]
    inter = [b_c[h] + m_prev[h] for h in hs]
    dmat = [jnp.where(incl, b_c[h] + r_row[h], -jnp.inf) for h in hs]
    mt = [jnp.maximum(inter[h], jnp.max(dmat[h], axis=1, keepdims=True)) for h in hs]
    s = [qk[h] * jnp.exp(dmat[h] - mt[h]) for h in hs]
    numden = [_dot(s[h], vp[h]) + jnp.exp(inter[h] - mt[h]) * qc[h] for h in hs]
    for h in hs:
        num = numden[h][:, 0:64]
        den = numden[h][:, 64:65]
        h_ref[:, h * 64:(h + 1) * 64] = num / jnp.maximum(jnp.abs(den), jnp.exp(-mt[h]))

    b_last = [jnp.sum(jnp.where(lane_r == end_lane, b_r[h], 0.0), axis=1, keepdims=True)
              for h in hs]
    m_new = [jnp.maximum(b_last[h] + m_prev[h], jnp.max(b_last[h] + r_row[h], axis=1, keepdims=True))
             for h in hs]
    kw = [k[h] * jnp.exp(b_last[h] + (i_c[h] - b_c[h]) - m_new[h]) for h in hs]
    upd = [_dot_tn(kw[h], vp[h]) for h in hs]
    for h in hs:
        c_s[h] = jnp.exp(b_last[h] + m_prev[h] - m_new[h]) * cp[h] + upd[h]
        m_s[h] = jnp.broadcast_to(m_new[h], (1, 128))

    @pl.when(j == nblk - 1)
    def _():
        cfin_ref[...] = c_s[...]
        mfin_ref[...] = m_s[...]


def _mlstm(p, gt, c0p, m0p, br, bc, nb, t, tok0):
    nblk = t // SEQ_BLK
    blk = _seq_specs(nblk, tok0 // SEQ_BLK)
    blk_out = _seq_specs(nblk, 0)
    ntok = nb * t
    return pl.pallas_call(
        functools.partial(_mlstm_kernel, nblk=nblk),
        out_shape=(jax.ShapeDtypeStruct((2, ntok, 256), F32),
                   jax.ShapeDtypeStruct((nb, 2, ML_HEADS, 64, 128), F32),
                   jax.ShapeDtypeStruct((nb, 2, ML_HEADS, 1, 128), F32)),
        grid=(nb, 2, nblk),
        in_specs=[
            pl.BlockSpec((SEQ_BLK, 768), lambda b, d, j: (blk(b, d, j), P_ML // 768)),
            pl.BlockSpec((None, 16, SEQ_BLK), lambda b, d, j: (d, 0, blk(b, d, j))),
            pl.BlockSpec((SEQ_BLK, 128), lambda b, d, j: (blk(b, d, j), P_GATE // 128 + d)),
            pl.BlockSpec((None, None, ML_HEADS, 64, 128), lambda b, d, j: (b, d, 0, 0, 0)),
            pl.BlockSpec((None, None, ML_HEADS, 1, 128), lambda b, d, j: (b, d, 0, 0, 0)),
            pl.BlockSpec((None, 8, 1), lambda b, d, j: (d, 0, 0)),
            pl.BlockSpec((None, 1, 128), lambda b, d, j: (d, 0, 0)),
        ],
        out_specs=(pl.BlockSpec((None, SEQ_BLK, 256), lambda b, d, j: (d, blk_out(b, d, j), 0)),
                   pl.BlockSpec((None, None, ML_HEADS, 64, 128), lambda b, d, j: (b, d, 0, 0, 0)),
                   pl.BlockSpec((None, None, ML_HEADS, 1, 128), lambda b, d, j: (b, d, 0, 0, 0))),
        scratch_shapes=[pltpu.VMEM((ML_HEADS, 64, 128), F32), pltpu.VMEM((ML_HEADS, 1, 128), F32)],
        compiler_params=_cparams(("parallel", "arbitrary", "arbitrary")),
        name="mlstm",
    )(p, gt, p, c0p, m0p, br, bc)


def _gdn_pre_kernel(x_ref, w_ref, o_ref, pad_s, *, t):
    pad_s[0:8, :] = jnp.zeros((8, 768), F32)
    pad_s[t + 8:t + 16, :] = jnp.zeros((8, 768), F32)
    pad_s[8:t + 8, :] = x_ref[...]
    y = jnp.zeros((t, 768), F32)
    for jj in range(DN_CONV):
        off = 8 + jj - DN_CONV // 2
        y = y + pad_s[off:off + t, :] * w_ref[jj:jj + 1, :]
    a = y * _sigmoid(y)
    for h in range(DN_HEADS):
        for part, mul in ((0, 64 ** -0.5), (256, 1.0)):
            sl = slice(part + h * 64, part + (h + 1) * 64)
            z = a[:, sl]
            o_ref[:, sl] = z * lax.rsqrt(jnp.sum(z * z, axis=1, keepdims=True) + NORM_EPS) * mul
    o_ref[:, 512:768] = a[:, 512:768]


def _gdn_pre(p, conv_w, nb, t, tok0):
    b0 = tok0 // t
    return pl.pallas_call(
        functools.partial(_gdn_pre_kernel, t=t),
        out_shape=jax.ShapeDtypeStruct((nb * t, 768), F32),
        grid=(nb,),
        in_specs=[
            pl.BlockSpec((t, 768), lambda b: (b0 + b, P_DN // 768)),
            pl.BlockSpec((8, 768), lambda b: (0, 0)),
        ],
        out_specs=pl.BlockSpec((t, 768), lambda b: (b, 0)),
        scratch_shapes=[pltpu.VMEM((t + 16, 768), F32)],
        compiler_params=_cparams(("parallel",)),
        name="gdn_pre",
    )(p, conv_w)


def _gdn_kernel(qkv_ref, gt_ref, gc_ref, s0_ref, pr_ref, pc_ref, o_ref, sfin_ref, s_s, *, nblk):
    n = SEQ_BLK
    d = pl.program_id(1)
    j = pl.program_id(2)

    @pl.when(j == 0)
    def _():
        s_s[...] = s0_ref[...]

    row, col, u = _dir_masks(d, n)
    incl = u <= 0
    strict = u < 0
    tri_c = jnp.where(incl, 1.0, 0.0).astype(F32)
    tri_r = jnp.where(u >= 0, 1.0, 0.0).astype(F32)
    end_lane = jnp.where(d == 0, n - 1, 0)
    eye = jnp.where(u == 0, 1.0, 0.0).astype(F32)

    a_r = gt_ref[8:16, :]
    g_r = -jnp.exp(pr_ref[0:8, :]) * _softplus(a_r + pr_ref[8:16, :])
    gcum_r = _dot_f32(g_r, tri_r)
    a_c = gc_ref[...]
    g_c = -jnp.exp(pc_ref[1:2, :]) * _softplus(a_c + pc_ref[0:1, :])
    gcum_c = _dot_f32(tri_c, g_c)
    beta_c = _sigmoid(a_c)
    lane_r = lax.broadcasted_iota(jnp.int32, (1, n), 1)

    hs = range(DN_HEADS)
    qn = [qkv_ref[:, h * 64:(h + 1) * 64] for h in hs]
    kn = [qkv_ref[:, 256 + h * 64:256 + (h + 1) * 64] for h in hs]
    vv = [qkv_ref[:, 512 + h * 64:512 + (h + 1) * 64] for h in hs]
    gc_row = [gcum_r[h:h + 1, :] for h in hs]
    gc_col = [gcum_c[:, 8 + h:9 + h] for h in hs]
    beta = [beta_c[:, 12 + h:13 + h] for h in hs]
    s_prev = [s_s[h] for h in hs]

    decay = [jnp.exp(jnp.where(incl, gc_col[h] - gc_row[h], -jnp.inf)) for h in hs]
    kb = [kn[h] * beta[h] for h in hs]
    gram = [_dot_nt(kb[h], kn[h]) for h in hs]
    a_mat = [jnp.where(strict, gram[h] * decay[h], 0.0) for h in hs]
    e_col = [jnp.exp(gc_col[h]) for h in hs]
    rhs = [jnp.concatenate([vv[h] * beta[h], kb[h] * e_col[h]], axis=1) for h in hs]

    same2 = (row >> 1) == (col >> 1)
    tinv = [eye - jnp.where(same2, a_mat[h], 0.0) for h in hs]
    for lb in range(1, 8):
        inner = (row >> lb) == (col >> lb)
        outer = (row >> (lb + 1)) == (col >> (lb + 1))
        off = jnp.logical_and(outer, jnp.logical_not(inner))
        m1 = [_dot(jnp.where(off, a_mat[h], 0.0), tinv[h]) for h in hs]
        tinv = [tinv[h] - _dot(tinv[h], m1[h]) for h in hs]
    sol = [_dot(tinv[h], rhs[h]) for h in hs]
    qk = [_dot_nt(qn[h], kn[h]) * decay[h] for h in hs]
    g_last = [jnp.sum(jnp.where(lane_r == end_lane, gc_row[h], 0.0), axis=1, keepdims=True)
              for h in hs]
    v_new = [sol[h][:, 0:64] - _dot(sol[h][:, 64:128], s_prev[h]) for h in hs]
    for h in hs:
        o_ref[:, h * 64:(h + 1) * 64] = _dot(qn[h] * e_col[h], s_prev[h]) + _dot(qk[h], v_new[h])
    for h in hs:
        k_dec = kn[h] * jnp.exp(g_last[h] - gc_col[h])
        s_s[h] = jnp.exp(g_last[h]) * s_prev[h] + _dot_tn(k_dec, v_new[h])

    @pl.when(j == nblk - 1)
    def _():
        sfin_ref[...] = s_s[...]


def _gdn(qkv, p, gt, s0, pr, pc, nb, t, tok0):
    nblk = t // SEQ_BLK
    blk = _seq_specs(nblk, tok0 // SEQ_BLK)
    blk0 = _seq_specs(nblk, 0)
    ntok = nb * t
    return pl.pallas_call(
        functools.partial(_gdn_kernel, nblk=nblk),
        out_shape=(jax.ShapeDtypeStruct((2, ntok, 256), F32),
                   jax.ShapeDtypeStruct((nb, 2, DN_HEADS, 64, 64), F32)),
        grid=(nb, 2, nblk),
        in_specs=[
            pl.BlockSpec((SEQ_BLK, 768), lambda b, d, j: (blk0(b, d, j), 0)),
            pl.BlockSpec((None, 16, SEQ_BLK), lambda b, d, j: (d, 0, blk(b, d, j))),
            pl.BlockSpec((SEQ_BLK, 128), lambda b, d, j: (blk(b, d, j), P_GATE // 128 + d)),
            pl.BlockSpec((None, None, DN_HEADS, 64, 64), lambda b, d, j: (b, d, 0, 0, 0)),
            pl.BlockSpec((None, 16, 1), lambda b, d, j: (d, 0, 0)),
            pl.BlockSpec((None, 2, 128), lambda b, d, j: (d, 0, 0)),
        ],
        out_specs=(pl.BlockSpec((None, SEQ_BLK, 256), lambda b, d, j: (d, blk0(b, d, j), 0)),
                   pl.BlockSpec((None, None, DN_HEADS, 64, 64), lambda b, d, j: (b, d, 0, 0, 0))),
        scratch_shapes=[pltpu.VMEM((DN_HEADS, 64, 64), F32)],
        compiler_params=_cparams(("parallel", "arbitrary", "arbitrary")),
        name="gdn",
    )(qkv, gt, p, s0, pr, pc)


MG_TM = 512


def _merge_kernel(x_ref, sh_ref, sc_ref, gt_ref, ona_ref, osw_ref, hf_ref, hb_ref, mlo_ref,
                  df_ref, db_ref, dng_ref, mlg_ref, dngain_ref, wmg_ref, wb_ref, wo_ref,
                  lng_ref, lnb_ref, o_ref):
    x = x_ref[...]
    xm = (x * (1.0 + sc_ref[...]) + sh_ref[...]).astype(BF16)
    hh = hf_ref[...] + hb_ref[...]
    dd = df_ref[...] + db_ref[...]
    ml_parts = []
    dn_parts = []
    for h in range(ML_HEADS):
        sl = slice(h * 64, (h + 1) * 64)
        z = hh[:, sl]
        mu = jnp.mean(z, axis=1, keepdims=True)
        zc = z - mu
        var = jnp.mean(zc * zc, axis=1, keepdims=True)
        ml_parts.append(zc * lax.rsqrt(var + LN_EPS))
        z = dd[:, sl]
        dn_parts.append(z * lax.rsqrt(jnp.mean(z * z, axis=1, keepdims=True) + NORM_EPS))
    o_ml = jnp.concatenate(ml_parts, axis=1) * mlg_ref[...] * _sigmoid(mlo_ref[...])
    g_pre = dng_ref[...]
    o_dn = jnp.concatenate(dn_parts, axis=1) * dngain_ref[...] * (g_pre * _sigmoid(g_pre))
    acc = jnp.zeros((x.shape[0], D_MODEL), F32)
    for nbr, o_n in enumerate((ona_ref[...], o_ml, o_dn, osw_ref[...])):
        y = _dot(o_n, wb_ref[nbr])
        gate = _sigmoid(jnp.dot(xm, wmg_ref[:, nbr * D_MODEL:(nbr + 1) * D_MODEL],
                                preferred_element_type=F32))
        acc = acc + gate * y
    mix = _dot(acc, wo_ref[...])
    o_ref[...] = _layer_norm(ALPHA * x + gt_ref[...] * mix, lng_ref[...], lnb_ref[...])


def _merge(x, mod_l, p, o_na, o_sw, hml, odn, ml_gain, dn_gain, wmg, wb, wo, ln_g, ln_b):
    tm = MG_TM
    row = lambda i: (i, 0)
    const2 = lambda i: (0, 0)
    return pl.pallas_call(
        _merge_kernel,
        out_shape=jax.ShapeDtypeStruct((N_TOK, D_MODEL), F32),
        grid=(N_TOK // tm,),
        in_specs=[
            pl.BlockSpec((tm, D_MODEL), row),
            _mod_spec(0, tm), _mod_spec(1, tm), _mod_spec(2, tm),
            pl.BlockSpec((tm, 256), row),
            pl.BlockSpec((tm, 256), row),
            pl.BlockSpec((None, tm, 256), lambda i: (0, i, 0)),
            pl.BlockSpec((None, tm, 256), lambda i: (1, i, 0)),
            pl.BlockSpec((tm, 256), lambda i: (i, P_MLO // 256)),
            pl.BlockSpec((None, tm, 256), lambda i: (0, i, 0)),
            pl.BlockSpec((None, tm, 256), lambda i: (1, i, 0)),
            pl.BlockSpec((tm, 256), lambda i: (i, P_DNG // 256)),
            pl.BlockSpec((1, 256), const2),
            pl.BlockSpec((1, 256), const2),
            pl.BlockSpec((D_MODEL, N_BRANCH * D_MODEL), const2),
            pl.BlockSpec((N_BRANCH, BRANCH_W, D_MODEL), lambda i: (0, 0, 0)),
            pl.BlockSpec((D_MODEL, D_MODEL), const2),
            pl.BlockSpec((1, D_MODEL), const2),
            pl.BlockSpec((1, D_MODEL), const2),
        ],
        out_specs=pl.BlockSpec((tm, D_MODEL), row),
        compiler_params=_cparams(("parallel",)),
        name="merge",
    )(x, mod_l, mod_l, mod_l, o_na, o_sw, hml, hml, p, odn, odn, p, ml_gain, dn_gain,
      wmg, wb, wo, ln_g, ln_b)


FF_TM = 512
FF_TH = 1408
FF_NK = FFN_DIM // FF_TH


def _ffn_kernel(x_ref, sh_ref, sc_ref, gt_ref, wg_ref, wu_ref, wo_ref, lng_ref, lnb_ref, o_ref,
                xm_s, acc_s):
    k = pl.program_id(1)

    @pl.when(k == 0)
    def _():
        xm_s[...] = (x_ref[...] * (1.0 + sc_ref[...]) + sh_ref[...]).astype(BF16)
        acc_s[...] = jnp.zeros_like(acc_s)

    xm = xm_s[...]
    hg = jnp.dot(xm, wg_ref[...], preferred_element_type=F32)
    hu = jnp.dot(xm, wu_ref[...], preferred_element_type=F32)
    a = hg * _sigmoid(hg) * hu
    acc_s[...] += _dot(a, wo_ref[...])

    @pl.when(k == FF_NK - 1)
    def _():
        z = ALPHA * x_ref[...] + gt_ref[...] * acc_s[...]
        o_ref[...] = _layer_norm(z, lng_ref[...], lnb_ref[...])


def _ffn(x, mod_l, w_in, w_out, ln_g, ln_b):
    tm = FF_TM
    return pl.pallas_call(
        _ffn_kernel,
        out_shape=jax.ShapeDtypeStruct((N_TOK, D_MODEL), F32),
        grid=(N_TOK // tm, FF_NK),
        in_specs=[
            pl.BlockSpec((tm, D_MODEL), lambda i, k: (i, 0)),
            _mod_spec(3, tm), _mod_spec(4, tm), _mod_spec(5, tm),
            pl.BlockSpec((D_MODEL, FF_TH), lambda i, k: (0, k)),
            pl.BlockSpec((D_MODEL, FF_TH), lambda i, k: (0, FF_NK + k)),
            pl.BlockSpec((FF_TH, D_MODEL), lambda i, k: (k, 0)),
            pl.BlockSpec((1, D_MODEL), lambda i, k: (0, 0)),
            pl.BlockSpec((1, D_MODEL), lambda i, k: (0, 0)),
        ],
        out_specs=pl.BlockSpec((tm, D_MODEL), lambda i, k: (i, 0)),
        scratch_shapes=[pltpu.VMEM((tm, D_MODEL), BF16), pltpu.VMEM((tm, D_MODEL), F32)],
        compiler_params=_cparams(("parallel", "arbitrary")),
        name="ffn",
    )(x, mod_l, mod_l, mod_l, w_in, w_in, w_out, ln_g, ln_b)


def _router_kernel(x_ref, sh_ref, sc_ref, wr_ref, br_ref, g_ref, xm_ref):
    xm = x_ref[...] * (1.0 + sc_ref[...]) + sh_ref[...]
    logits = _dot_f32(xm, wr_ref[...]) + br_ref[...]
    lane = lax.broadcasted_iota(jnp.int32, logits.shape, 1)
    neg = jnp.where(lane < N_EXPERTS, logits, -jnp.inf)
    v1 = jnp.max(neg, axis=1, keepdims=True)
    i1 = jnp.min(jnp.where(neg == v1, lane, 128), axis=1, keepdims=True)
    rest = jnp.where(lane == i1, -jnp.inf, neg)
    v2 = jnp.max(rest, axis=1, keepdims=True)
    i2 = jnp.min(jnp.where(rest == v2, lane, 128), axis=1, keepdims=True)
    e2 = jnp.exp(v2 - v1)
    p1 = 1.0 / (1.0 + e2)
    p2 = e2 / (1.0 + e2)
    g_ref[...] = (jnp.where(lane == 0, i1.astype(F32), 0.0) + jnp.where(lane == 1, i2.astype(F32), 0.0)
                  + jnp.where(lane == 2, p1, 0.0) + jnp.where(lane == 3, p2, 0.0))
    xm_ref[...] = xm


def _router(x, mod_l, wr, br):
    tm = 512
    return pl.pallas_call(
        _router_kernel,
        out_shape=(jax.ShapeDtypeStruct((N_TOK, 128), F32),
                   jax.ShapeDtypeStruct((N_TOK, D_MODEL), F32)),
        grid=(N_TOK // tm,),
        in_specs=[
            pl.BlockSpec((tm, D_MODEL), lambda i: (i, 0)),
            _mod_spec(3, tm), _mod_spec(4, tm),
            pl.BlockSpec((D_MODEL, 128), lambda i: (0, 0)),
            pl.BlockSpec((1, 128), lambda i: (0, 0)),
        ],
        out_specs=(pl.BlockSpec((tm, 128), lambda i: (i, 0)),
                   pl.BlockSpec((tm, D_MODEL), lambda i: (i, 0))),
        compiler_params=_cparams(("parallel",)),
        name="router",
    )(x, mod_l, mod_l, wr, br)


GATHER_ROWS = 256
GATHER_UNROLL = 8


def _gather_kernel(idx_ref, src_ref, o_ref, sem):
    base = pl.program_id(0) * GATHER_ROWS

    def _copy(r, src_row):
        return pltpu.make_async_copy(src_ref.at[pl.ds(src_row, 1)], o_ref.at[pl.ds(r, 1)], sem)

    def _start(g, carry):
        for u in range(GATHER_UNROLL):
            r = g * GATHER_UNROLL + u
            _copy(r, idx_ref[base + r]).start(priority=u % 2)
        return carry

    def _wait(r, carry):
        _copy(r, 0).wait()
        return carry

    lax.fori_loop(0, GATHER_ROWS // GATHER_UNROLL, _start, 0)
    lax.fori_loop(0, GATHER_ROWS, _wait, 0, unroll=GATHER_UNROLL)


def _row_gather(src, idx):
    m = idx.shape[0]
    return pl.pallas_call(
        _gather_kernel,
        out_shape=jax.ShapeDtypeStruct((m, src.shape[1]), src.dtype),
        grid_spec=pltpu.PrefetchScalarGridSpec(
            num_scalar_prefetch=1,
            grid=(m // GATHER_ROWS,),
            in_specs=[pl.BlockSpec(memory_space=pl.ANY)],
            out_specs=pl.BlockSpec((GATHER_ROWS, src.shape[1]), lambda i, idx_ref: (i, 0)),
            scratch_shapes=[pltpu.SemaphoreType.DMA(())],
        ),
        compiler_params=_cparams(("arbitrary",)),
        name="row_gather",
    )(idx, src)


MOE_TM = 512
MOE_NT = 2 * N_TOK // MOE_TM + N_EXPERTS


def _route_plan(aux):
    e = jnp.concatenate([aux[:, 0], aux[:, 1]]).astype(jnp.int32)
    tok = jnp.tile(jnp.arange(N_TOK, dtype=jnp.int32), 2)
    onehot = (e[:, None] == jnp.arange(N_EXPERTS, dtype=jnp.int32)[None, :]).astype(jnp.int32)
    csum = jnp.cumsum(onehot, axis=0)
    rank = jnp.sum((csum - onehot) * onehot, axis=1)
    counts = csum[-1]
    tiles_per = (counts + MOE_TM - 1) // MOE_TM
    tile_end = jnp.cumsum(tiles_per)
    tile_start = tile_end - tiles_per
    dest = jnp.sum(onehot * tile_start[None, :], axis=1) * MOE_TM + rank
    row_src = jnp.zeros((MOE_NT * MOE_TM,), jnp.int32).at[dest].set(tok)
    n_used = tile_end[-1]
    tiles = jnp.minimum(jnp.arange(MOE_NT, dtype=jnp.int32), n_used - 1)
    tile_expert = jnp.sum((tiles[:, None] >= tile_end[None, :]).astype(jnp.int32), axis=1)
    pos = jnp.stack([dest[:N_TOK], dest[N_TOK:]], axis=1).reshape(2 * N_TOK)
    return row_src, pos, tile_expert.astype(jnp.int32), n_used.reshape(1).astype(jnp.int32)


def _moe_ffn_kernel(te_ref, nu_ref, x_ref, wg_ref, wu_ref, wo_ref, o_ref, xm_s, acc_s):
    i = pl.program_id(0)
    k = pl.program_id(1)
    used = i < nu_ref[0]

    @pl.when(jnp.logical_and(used, k == 0))
    def _():
        xm_s[...] = x_ref[...].astype(BF16)
        acc_s[...] = jnp.zeros_like(acc_s)

    @pl.when(used)
    def _():
        xm = xm_s[...]
        hg = jnp.dot(xm, wg_ref[...], preferred_element_type=F32)
        hu = jnp.dot(xm, wu_ref[...], preferred_element_type=F32)
        a = hg * _sigmoid(hg) * hu
        acc_s[...] += _dot(a, wo_ref[...])

    @pl.when(k == FF_NK - 1)
    def _():
        o_ref[...] = jnp.where(used, acc_s[...], 0.0)


def _moe_ffn(xg, tile_expert, n_used, w_in, w_out):
    tm = MOE_TM

    def kk(i, k, nu):
        return jnp.where(i < nu[0], k, FF_NK - 1)

    return pl.pallas_call(
        _moe_ffn_kernel,
        out_shape=jax.ShapeDtypeStruct((MOE_NT * tm, D_MODEL), F32),
        grid_spec=pltpu.PrefetchScalarGridSpec(
            num_scalar_prefetch=2,
            grid=(MOE_NT, FF_NK),
            in_specs=[
                pl.BlockSpec((tm, D_MODEL), lambda i, k, te, nu: (i, 0)),
                pl.BlockSpec((None, D_MODEL, FF_TH), lambda i, k, te, nu: (te[i], 0, kk(i, k, nu))),
                pl.BlockSpec((None, D_MODEL, FF_TH),
                             lambda i, k, te, nu: (te[i], 0, FF_NK + kk(i, k, nu))),
                pl.BlockSpec((None, FF_TH, D_MODEL), lambda i, k, te, nu: (te[i], kk(i, k, nu), 0)),
            ],
            out_specs=pl.BlockSpec((tm, D_MODEL), lambda i, k, te, nu: (i, 0)),
            scratch_shapes=[pltpu.VMEM((tm, D_MODEL), BF16), pltpu.VMEM((tm, D_MODEL), F32)],
        ),
        compiler_params=_cparams(("arbitrary", "arbitrary")),
        name="moe_ffn",
    )(tile_expert, n_used, xg, w_in, w_in, w_out)


def _moe_finish_kernel(x_ref, gt_ref, aux_ref, y_ref, lng_ref, lnb_ref, o_ref):
    aux = aux_ref[...]
    ff = aux[:, 2:3] * y_ref[:, 0:D_MODEL] + aux[:, 3:4] * y_ref[:, D_MODEL:2 * D_MODEL]
    z = ALPHA * x_ref[...] + gt_ref[...] * ff
    o_ref[...] = _layer_norm(z, lng_ref[...], lnb_ref[...])


def _moe_finish(x, mod_l, aux, yg, ln_g, ln_b):
    tm = 512
    return pl.pallas_call(
        _moe_finish_kernel,
        out_shape=jax.ShapeDtypeStruct((N_TOK, D_MODEL), F32),
        grid=(N_TOK // tm,),
        in_specs=[
            pl.BlockSpec((tm, D_MODEL), lambda i: (i, 0)),
            _mod_spec(5, tm),
            pl.BlockSpec((tm, 128), lambda i: (i, 0)),
            pl.BlockSpec((tm, 2 * D_MODEL), lambda i: (i, 0)),
            pl.BlockSpec((1, D_MODEL), lambda i: (0, 0)),
            pl.BlockSpec((1, D_MODEL), lambda i: (0, 0)),
        ],
        out_specs=pl.BlockSpec((tm, D_MODEL), lambda i: (i, 0)),
        compiler_params=_cparams(("parallel",)),
        name="moe_finish",
    )(x, mod_l, aux, yg, ln_g, ln_b)


def _split_w_in(w):
    na = w[:, 0:768]
    ml = w[:, 768:1536]
    mlg = w[:, 1536:1552]
    mlo = w[:, 1552:1808]
    dn = w[:, 1808:2576]
    dna = w[:, 2576:2584]
    dnb = w[:, 2584:2592]
    dng = w[:, 2592:2848]
    sw = w[:, 2848:3360]
    mg = w[:, 3360:7456]
    zpad = jnp.zeros((D_MODEL, 112), w.dtype)
    gates = []
    for d in range(2):
        gates.append(jnp.concatenate([mlg[:, 4 * d:4 * d + 4], mlg[:, 8 + 4 * d:12 + 4 * d],
                                      dna[:, 4 * d:4 * d + 4], dnb[:, 4 * d:4 * d + 4]], axis=1))
    w1 = jnp.concatenate([na, ml, dn, mlo, dng, sw, gates[0], zpad, gates[1], zpad], axis=1)
    wt = jnp.concatenate(gates, axis=1).T
    return w1.astype(BF16), wt.astype(BF16), mg.astype(BF16)


def kernel(x_prompt, x_sample, c, cache_na_k, cache_na_v, cache_sw_k, cache_sw_v, state_ml_c, state_ml_n, state_ml_m, state_dn_s, c_ctx, ada_w, ada_b, w_in, na_rpb, ml_gate_b, ml_norm_g, dn_conv_w, dn_a_log, dn_dt_bias, dn_norm_g, sw_sink, w_branch, w_out, ln_g, ln_b, ffn_w_in, ffn_w_out, moe_router, moe_router_b, moe_w_in, moe_w_out):
    x = jnp.concatenate([x_prompt.reshape(N_CTX, D_MODEL), x_sample.reshape(N_LAT, D_MODEL)], axis=0)
    cvec = jnp.concatenate([c_ctx[None, :], c, jnp.zeros((3, D_MODEL), F32)], axis=0)
    mod = _ada(cvec, ada_w, ada_b).reshape(DEPTH, 8, 6, 1, D_MODEL)
    cos, sin = _rope_tables()

    zc_ctx = jnp.zeros((BATCH, 2, ML_HEADS, 64, 128), F32)
    zm_ctx = jnp.zeros((BATCH, 2, ML_HEADS, 1, 128), F32)
    zs_ctx = jnp.zeros((BATCH, 2, DN_HEADS, 64, 64), F32)
    pad63 = jnp.zeros((DEC_BATCH, 2, ML_HEADS, 64, 63), F32)

    new = [[] for _ in range(8)]
    for l in range(DEPTH):
        w1, wt, wmg = _split_w_in(w_in[l])
        mod_l = mod[l]
        p, gt = _inproj(x, mod_l, w1, wt)

        o_na_c, o_sw_c = _ctx_attn(p, sw_sink[l])
        ck = cache_na_k[:, l].reshape(DEC_BATCH, PAST_LEN, 256)
        cv = cache_na_v[:, l].reshape(DEC_BATCH, PAST_LEN, 256)
        o_na_l = _lat_na(p, ck, cv, _na_bias_table(na_rpb[l]))
        sk = cache_sw_k[:, l].reshape(DEC_BATCH, PAST_LEN, 128)
        sv = cache_sw_v[:, l].reshape(DEC_BATCH, PAST_LEN, 128)
        o_sw_l = _lat_swa(p, sw_sink[l], sk, sv, cos, sin)

        gb = ml_gate_b[l]
        br = jnp.stack([jnp.concatenate([gb[0], gb[2]]), jnp.concatenate([gb[1], gb[3]])])[:, :, None]
        bc = jnp.pad(br[:, :, 0], ((0, 0), (0, 120)))[:, None, :]
        h_c, cfin, mfin = _mlstm(p, gt, zc_ctx, zm_ctx, br, bc, BATCH, SEQ, 0)
        c0p = jnp.concatenate([state_ml_c[:, l], state_ml_n[:, l][..., None], pad63], axis=-1)
        m0p = jnp.broadcast_to(state_ml_m[:, l][..., None, None], (DEC_BATCH, 2, ML_HEADS, 1, 128))
        h_l, _, _ = _mlstm(p, gt, c0p, m0p, br, bc, DEC_BATCH, DEC_SEQ, N_CTX)

        conv_w = jnp.pad(dn_conv_w[l], ((0, 3), (0, 0)))
        z4 = jnp.zeros((2, 4), F32)
        pr = jnp.concatenate([dn_a_log[l], z4, dn_dt_bias[l], z4], axis=1)[:, :, None]
        pc = jnp.stack([jnp.pad(dn_dt_bias[l], ((0, 0), (8, 116))),
                        jnp.pad(dn_a_log[l], ((0, 0), (8, 116)))], axis=1)
        qkv_c = _gdn_pre(p, conv_w, BATCH, SEQ, 0)
        o_c, sfin = _gdn(qkv_c, p, gt, zs_ctx, pr, pc, BATCH, SEQ, 0)
        qkv_l = _gdn_pre(p, conv_w, DEC_BATCH, DEC_SEQ, N_CTX)
        o_l, _ = _gdn(qkv_l, p, gt, state_dn_s[:, l], pr, pc, DEC_BATCH, DEC_SEQ, N_CTX)

        o_na = jnp.concatenate([o_na_c, o_na_l], axis=0)
        o_sw = jnp.concatenate([o_sw_c, o_sw_l], axis=0)
        hml = jnp.concatenate([h_c, h_l], axis=1)
        odn = jnp.concatenate([o_c, o_l], axis=1)
        x = _merge(x, mod_l, p, o_na, o_sw, hml, odn, ml_norm_g[l][None, :],
                   jnp.tile(dn_norm_g[l], DN_HEADS)[None, :], wmg, w_branch[l].astype(BF16),
                   w_out[l].astype(BF16), ln_g[l, 0][None, :], ln_b[l, 0][None, :])

        jx = l // 2
        if l % 2 == 0:
            x = _ffn(x, mod_l, ffn_w_in[jx].astype(BF16), ffn_w_out[jx].astype(BF16),
                     ln_g[l, 1][None, :], ln_b[l, 1][None, :])
        else:
            wr = jnp.pad(moe_router[jx], ((0, 0), (0, 120)))
            brt = jnp.pad(moe_router_b[jx], (0, 120))[None, :]
            aux, xm = _router(x, mod_l, wr, brt)
            row_src, pos, tile_expert, n_used = _route_plan(aux)
            xg = _row_gather(xm, row_src)
            y = _moe_ffn(xg, tile_expert, n_used, moe_w_in[jx].astype(BF16),
                         moe_w_out[jx].astype(BF16))
            yg = _row_gather(y, pos).reshape(N_TOK, 2 * D_MODEL)
            x = _moe_finish(x, mod_l, aux, yg, ln_g[l, 1][None, :], ln_b[l, 1][None, :])

        pc_ = p[0:N_CTX]
        new[0].append(pc_[:, 256:512].reshape(BATCH, SEQ, NA_HEADS, HEAD_DIM))
        new[1].append(pc_[:, 512:768].reshape(BATCH, SEQ, NA_HEADS, HEAD_DIM))
        new[2].append(pc_[:, P_SWK:P_SWK + 128].reshape(BATCH, SEQ, SW_KV_HEADS, HEAD_DIM))
        new[3].append(pc_[:, P_SWV:P_SWV + 128].reshape(BATCH, SEQ, SW_KV_HEADS, HEAD_DIM))
        new[4].append(cfin[..., 0:64])
        new[5].append(cfin[..., 64])
        new[6].append(mfin[..., 0, 0])
        new[7].append(sfin)

    y_prompt = x[0:N_CTX].reshape(BATCH, SEQ, D_MODEL)
    y_sample = x[N_CTX:].reshape(DEC_BATCH, DEC_SEQ, D_MODEL)
    outs = [jnp.stack(s, axis=1) for s in new]
    return (y_prompt, y_sample) + tuple(outs)
```
